```python
import math
import jax, jax.numpy as jnp
from jax import lax
import numpy as np

D_MODEL = 2048
BATCH = 8
SEQ = 2048
DEPTH = 1

N_META = 16
ATTN_HEADS = 8
HEAD_DIM = 64
V_HEAD_DIM = 2 * HEAD_DIM
QK_WIDTH = ATTN_HEADS * 2 * HEAD_DIM
ATTN_WIDTH = ATTN_HEADS * V_HEAD_DIM
ROT_DIM = HEAD_DIM // 4
ROPE_THETA = 500000.0
CONV_WIDTH = D_MODEL // 2
CONV_KERNEL = 31
D_FF = 4 * D_MODEL
Q_BLOCK = 128
EPS = 1e-6
SPLITS = (QK_WIDTH, QK_WIDTH, ATTN_WIDTH, CONV_WIDTH, CONV_WIDTH, D_MODEL, D_MODEL)
IN_COLS = sum(SPLITS)

kernel_name = "hybrid_diffattn_conformer_gated_block"


def lambda_init_fn(layer_idx):
    return 0.8 - 0.6 * math.exp(-0.3 * layer_idx)


def rms_norm(x, g):
    xf = x.astype(jnp.float32)
    y = xf * lax.rsqrt(jnp.mean(xf * xf, axis=-1, keepdims=True) + EPS)
    return (y * g.astype(jnp.float32)).astype(x.dtype)


def layer_norm(x, g, b):
    xf = x.astype(jnp.float32)
    mu = jnp.mean(xf, axis=-1, keepdims=True)
    var = jnp.mean(jnp.square(xf - mu), axis=-1, keepdims=True)
    y = (xf - mu) * lax.rsqrt(var + EPS)
    return (y * g.astype(jnp.float32) + b.astype(jnp.float32)).astype(x.dtype)


def rope_tables(length):
    pos = jnp.arange(length, dtype=jnp.float32)
    inv_freq = ROPE_THETA ** (-jnp.arange(0, ROT_DIM, 2, dtype=jnp.float32) / ROT_DIM)
    ang = pos[:, None] * inv_freq[None, :]
    cos = jnp.concatenate([jnp.cos(ang), jnp.cos(ang)], axis=-1)
    sin = jnp.concatenate([jnp.sin(ang), jnp.sin(ang)], axis=-1)
    return cos, sin


def rope_partial(t, cos, sin):
    cos = cos.astype(t.dtype)
    sin = sin.astype(t.dtype)
    rot, rest = t[..., :ROT_DIM], t[..., ROT_DIM:]
    r1, r2 = rot[..., :ROT_DIM // 2], rot[..., ROT_DIM // 2:]
    rotated = jnp.concatenate([-r2, r1], axis=-1)
    return jnp.concatenate([rot * cos + rotated * sin, rest], axis=-1)


def diff_attention_causal(q, k, v, lam):
    lp = q.shape[3]
    scale = HEAD_DIM ** -0.5
    outs = []
    for i in range(lp // Q_BLOCK):
        s0, end = i * Q_BLOCK, (i + 1) * Q_BLOCK
        qb = q[:, :, :, s0:end]
        kb = k[:, :, :, :end]
        vb = v[:, :, :end]
        s = jnp.einsum('bhcqd,bhckd->bhcqk', qb, kb).astype(jnp.float32) * scale
        qpos = jnp.arange(s0, end)[:, None]
        kpos = jnp.arange(end)[None, :]
        s = jnp.where(kpos <= qpos, s, -jnp.inf)
        p = jax.nn.softmax(s, axis=-1)
        a = p[:, :, 0] - lam * p[:, :, 1]
        outs.append(jnp.einsum('bhqk,bhkv->bhqv', a.astype(v.dtype), vb))
    return jnp.concatenate(outs, axis=2)


def setup_inputs(seed: int = 0) -> dict:
    key = jax.random.key(seed)
    ks = jax.random.split(key, 24)
    f32 = jnp.float32
    nrm = lambda k, shape, s: jax.random.normal(k, shape, f32) * s
    return {
        "x": nrm(ks[0], (BATCH, SEQ, D_MODEL), 1.0),
        "meta": nrm(ks[1], (N_META, D_MODEL), 1.0),
        "norm1_g": 1.0 + nrm(ks[2], (DEPTH, D_MODEL), 0.02),
        "w_in": nrm(ks[3], (DEPTH, D_MODEL, IN_COLS), D_MODEL ** -0.5),
        "q_norm_g": 1.0 + nrm(ks[4], (DEPTH, HEAD_DIM), 0.02),
        "k_norm_g": 1.0 + nrm(ks[5], (DEPTH, HEAD_DIM), 0.02),
        "lambda_q1": nrm(ks[6], (DEPTH, HEAD_DIM), 0.1),
        "lambda_k1": nrm(ks[7], (DEPTH, HEAD_DIM), 0.1),
        "lambda_q2": nrm(ks[8], (DEPTH, HEAD_DIM), 0.1),
        "lambda_k2": nrm(ks[9], (DEPTH, HEAD_DIM), 0.1),
        "subln_g": 1.0 + nrm(ks[10], (DEPTH, V_HEAD_DIM), 0.02),
        "w_attn_o": nrm(ks[11], (DEPTH, ATTN_WIDTH, D_MODEL), ATTN_WIDTH ** -0.5),
        "dw_kernel": nrm(ks[12], (DEPTH, CONV_KERNEL, CONV_WIDTH), CONV_KERNEL ** -0.5),
        "dw_bias": nrm(ks[13], (DEPTH, CONV_WIDTH), 0.02),
        "conv_ln_g": 1.0 + nrm(ks[14], (DEPTH, CONV_WIDTH), 0.02),
        "conv_ln_b": nrm(ks[15], (DEPTH, CONV_WIDTH), 0.02),
        "w_conv_o": nrm(ks[16], (DEPTH, CONV_WIDTH, D_MODEL), CONV_WIDTH ** -0.5),
        "w_out": nrm(ks[17], (DEPTH, D_MODEL, D_MODEL), D_MODEL ** -0.5),
        "norm2_g": 1.0 + nrm(ks[18], (DEPTH, D_MODEL), 0.02),
        "w_up": nrm(ks[19], (DEPTH, D_MODEL, D_FF), D_MODEL ** -0.5),
        "w_down": nrm(ks[20], (DEPTH, D_FF, D_MODEL), D_FF ** -0.5),
    }


def reference(x, meta, norm1_g, w_in, q_norm_g, k_norm_g, lambda_q1, lambda_k1,
              lambda_q2, lambda_k2, subln_g, w_attn_o, dw_kernel, dw_bias,
              conv_ln_g, conv_ln_b, w_conv_o, w_out, norm2_g, w_up, w_down):
    b, s, d = x.shape
    h = jnp.concatenate([jnp.broadcast_to(meta[None].astype(x.dtype), (b, N_META, d)), x], axis=1)
    L = s + N_META
    Lp = ((L + Q_BLOCK - 1) // Q_BLOCK) * Q_BLOCK
    cos, sin = rope_tables(L)
    split_idx = [int(v) for v in np.cumsum(SPLITS)[:-1]]

    for l in range(DEPTH):
        lam_init = lambda_init_fn(l)
        u = rms_norm(h, norm1_g[l])
        proj = jnp.einsum('bld,dc->blc', u, w_in[l])
        q_all, k_all, v_all, ga, gb, g_attn, g_conv = jnp.split(proj, split_idx, axis=-1)

        q = q_all.reshape(b, L, ATTN_HEADS, 2, HEAD_DIM).transpose(0, 2, 3, 1, 4)
        k = k_all.reshape(b, L, ATTN_HEADS, 2, HEAD_DIM).transpose(0, 2, 3, 1, 4)
        v = v_all.reshape(b, L, ATTN_HEADS, V_HEAD_DIM).transpose(0, 2, 1, 3)
        q = rope_partial(rms_norm(q, q_norm_g[l]), cos, sin)
        k = rope_partial(rms_norm(k, k_norm_g[l]), cos, sin)
        pad = Lp - L
        q = jnp.pad(q, ((0, 0), (0, 0), (0, 0), (0, pad), (0, 0)))
        k = jnp.pad(k, ((0, 0), (0, 0), (0, 0), (0, pad), (0, 0)))
        v = jnp.pad(v, ((0, 0), (0, 0), (0, pad), (0, 0)))
        lam = (jnp.exp(jnp.sum(lambda_q1[l].astype(jnp.float32) * lambda_k1[l].astype(jnp.float32)))
               - jnp.exp(jnp.sum(lambda_q2[l].astype(jnp.float32) * lambda_k2[l].astype(jnp.float32)))
               + lam_init)
        o = diff_attention_causal(q, k, v, lam)[:, :, :L]
        o = rms_norm(o, subln_g[l]) * (1.0 - lam_init)
        o = o.transpose(0, 2, 1, 3).reshape(b, L, ATTN_WIDTH)
        y_attn = jnp.einsum('blc,cd->bld', o, w_attn_o[l])

        a = ga * jax.nn.sigmoid(gb)
        c = lax.conv_general_dilated(
            a, dw_kernel[l][:, None, :].astype(a.dtype), window_strides=(1,),
            padding=[(CONV_KERNEL - 1, 0)], dimension_numbers=('NWC', 'WIO', 'NWC'),
            feature_group_count=CONV_WIDTH) + dw_bias[l]
        c = jax.nn.silu(layer_norm(c, conv_ln_g[l], conv_ln_b[l]))
        y_conv = jnp.einsum('blc,cd->bld', c, w_conv_o[l])

        m = jax.nn.sigmoid(g_attn) * y_attn + jax.nn.sigmoid(g_conv) * y_conv
        h = h + jnp.einsum('bld,de->ble', m, w_out[l])

        u2 = rms_norm(h, norm2_g[l])
        z = jnp.square(jax.nn.relu(jnp.einsum('bld,df->blf', u2, w_up[l])))
        h = h + jnp.einsum('blf,fd->bld', z, w_down[l])

    return h[:, N_META:]
```

```python
import functools
import math

import jax
import jax.numpy as jnp
from jax import lax
from jax.experimental import pallas as pl
from jax.experimental.pallas import tpu as pltpu

F32 = jnp.float32
BF16 = jnp.bfloat16

D_MODEL = 2048
N_META = 16
HEADS = 8
HEAD_DIM = 64
V_HEAD_DIM = 2 * HEAD_DIM
QK_WIDTH = HEADS * 2 * HEAD_DIM
ATTN_WIDTH = HEADS * V_HEAD_DIM
ROT_DIM = HEAD_DIM // 4
ROPE_THETA = 500000.0
CONV_WIDTH = D_MODEL // 2
CONV_KERNEL = 31
D_FF = 4 * D_MODEL
EPS = 1e-6
LAM_INIT = 0.8 - 0.6 * math.exp(-0.3 * 0)

LANES = 128
VMEM_LIMIT = 56 * 1024 * 1024

PROJ_TN = 512
PACKED_COLS = 3 * QK_WIDTH + CONV_WIDTH + 2 * D_MODEL
N_PROJ_TILES = PACKED_COLS // PROJ_TN
GLU_TILE0 = (3 * QK_WIDTH) // PROJ_TN
GATE_TILE0 = GLU_TILE0 + CONV_WIDTH // PROJ_TN
GLU_B_WTILE0 = (3 * QK_WIDTH + CONV_WIDTH) // PROJ_TN
N_QK_TILES = (2 * QK_WIDTH) // PROJ_TN


def _sigmoid(x):
    return 1.0 / (1.0 + jnp.exp(-x))


def _proj_kernel(x_ref, g_ref, w1_ref, w2_ref, qkg_ref, cos_ref, sa_ref, sb_ref, gsum_ref,
                 o_ref, u_ref, *, j0, row_chunk):
    j = pl.program_id(1) + j0
    tm = x_ref.shape[0]

    @pl.when(pl.program_id(1) == 0)
    def _():
        def body(r, carry):
            rows = pl.ds(pl.multiple_of(r * row_chunk, row_chunk), row_chunk)
            x = x_ref[rows, :]
            ms = jnp.mean(x * x, axis=-1, keepdims=True)
            u_ref[rows, :] = (x * lax.rsqrt(ms + EPS) * g_ref[...]).astype(BF16)
            return carry
        lax.fori_loop(0, tm // row_chunk, body, 0)

    @pl.when(j < N_QK_TILES)
    def _():
        y = jnp.dot(u_ref[...], w1_ref[...], preferred_element_type=F32)
        for c in range(PROJ_TN // LANES):
            lanes = slice(c * LANES, (c + 1) * LANES)
            yc = y[:, lanes]
            yy = yc * yc
            hi = yy.astype(BF16)
            lo = (yy - hi.astype(F32)).astype(BF16)
            ss = (jnp.dot(hi, gsum_ref[...], preferred_element_type=F32)
                  + jnp.dot(lo, gsum_ref[...], preferred_element_type=F32))
            yn = yc * lax.rsqrt(ss * (1.0 / HEAD_DIM) + EPS) * qkg_ref[:, lanes]
            out = (yn * cos_ref[...]
                   + pltpu.roll(yn, ROT_DIM // 2, 1) * sa_ref[...]
                   + pltpu.roll(yn, LANES - ROT_DIM // 2, 1) * sb_ref[...])
            o_ref[:, lanes] = out.astype(BF16)

    @pl.when((j >= N_QK_TILES) & (j < GLU_TILE0))
    def _():
        y = jnp.dot(u_ref[...], w1_ref[...], preferred_element_type=F32)
        o_ref[...] = y.astype(BF16)

    @pl.when((j >= GLU_TILE0) & (j < GATE_TILE0))
    def _():
        ya = jnp.dot(u_ref[...], w1_ref[...], preferred_element_type=F32)
        yb = jnp.dot(u_ref[...], w2_ref[...], preferred_element_type=F32)
        o_ref[...] = (ya * _sigmoid(yb)).astype(BF16)

    @pl.when(j >= GATE_TILE0)
    def _():
        y = jnp.dot(u_ref[...], w1_ref[...], preferred_element_type=F32)
        o_ref[...] = _sigmoid(y).astype(BF16)


def _proj_call(x2, g, w_in_bf, qkg, cos_t, sa_t, sb_t, gsum, *, tm, j0, nj):
    m = x2.shape[0]
    npb = cos_t.shape[0] // tm

    def w1_map(i, j):
        jj = j + j0
        return (0, jnp.where(jj < GATE_TILE0, jj, jj + 2))

    def w2_map(i, j):
        return (0, jnp.clip(j + j0 + 2, GLU_B_WTILE0, GLU_B_WTILE0 + 1))

    def qkg_map(i, j):
        return (0, jnp.minimum(j + j0, N_QK_TILES - 1))

    kern = functools.partial(_proj_kernel, j0=j0, row_chunk=min(tm, 64))
    return pl.pallas_call(
        kern,
        grid=(m // tm, nj),
        in_specs=[
            pl.BlockSpec((tm, D_MODEL), lambda i, j: (i, 0)),
            pl.BlockSpec((1, D_MODEL), lambda i, j: (0, 0)),
            pl.BlockSpec((D_MODEL, PROJ_TN), w1_map),
            pl.BlockSpec((D_MODEL, PROJ_TN), w2_map),
            pl.BlockSpec((1, PROJ_TN), qkg_map),
            pl.BlockSpec((tm, LANES), lambda i, j: (i % npb, 0)),
            pl.BlockSpec((tm, LANES), lambda i, j: (i % npb, 0)),
            pl.BlockSpec((tm, LANES), lambda i, j: (i % npb, 0)),
            pl.BlockSpec((LANES, LANES), lambda i, j: (0, 0)),
        ],
        out_specs=pl.BlockSpec((tm, PROJ_TN), lambda i, j: (i, j)),
        out_shape=jax.ShapeDtypeStruct((m, nj * PROJ_TN), BF16),
        scratch_shapes=[pltpu.VMEM((tm, D_MODEL), BF16)],
        compiler_params=pltpu.CompilerParams(
            dimension_semantics=("arbitrary", "arbitrary"), vmem_limit_bytes=VMEM_LIMIT),
        name="proj",
    )(x2, g, w_in_bf, w_in_bf, qkg, cos_t, sa_t, sb_t, gsum)


def _attn_kernel(q_ref, k_ref, v_ref, km_ref, vm_ref, lq1_ref, lk1_ref, lq2_ref, lk2_ref, sg_ref,
                 o_ref, *, tq):
    seq = q_ref.shape[0]
    nq = seq // tq
    lam = (jnp.exp(jnp.sum(lq1_ref[...] * lk1_ref[...], axis=-1, keepdims=True))
           - jnp.exp(jnp.sum(lq2_ref[...] * lk2_ref[...], axis=-1, keepdims=True))
           + LAM_INIT)
    lane = lax.broadcasted_iota(jnp.int32, (tq, V_HEAD_DIM), 1)
    first_map = lane < HEAD_DIM
    r_io = lax.broadcasted_iota(jnp.int32, (2 * tq, tq), 0)
    c_io = lax.broadcasted_iota(jnp.int32, (2 * tq, tq), 1)
    causal = c_io <= jnp.where(r_io >= tq, r_io - tq, r_io)
    nt = (((1,), (1,)), ((), ()))

    def step(qq, kb, vb, carry, mask):
        m, l, acc = carry
        s = lax.dot_general(qq, kb, nt, preferred_element_type=F32)
        if mask is not None:
            s = jnp.where(mask, s, -jnp.inf)
        m_new = jnp.maximum(m, jnp.max(s, axis=-1, keepdims=True))
        alpha = jnp.exp(m - m_new)
        p = jnp.exp(s - m_new)
        l = alpha * l + jnp.sum(p, axis=-1, keepdims=True)
        acc = alpha * acc + jnp.dot(p.astype(BF16), vb, preferred_element_type=F32)
        return m_new, l, acc

    for qi in range(nq):
        q = q_ref[qi * tq:(qi + 1) * tq, :]
        zero = jnp.zeros_like(q)
        qq = jnp.concatenate([jnp.where(first_map, q, zero), jnp.where(first_map, zero, q)], axis=0)

        s = lax.dot_general(qq, km_ref[...], nt, preferred_element_type=F32)
        m = jnp.max(s, axis=-1, keepdims=True)
        p = jnp.exp(s - m)
        l = jnp.sum(p, axis=-1, keepdims=True)
        acc = jnp.dot(p.astype(BF16), vm_ref[...], preferred_element_type=F32)
        carry = (m, l, acc)

        def body(t, carry):
            rows = pl.ds(pl.multiple_of(t * tq, tq), tq)
            return step(qq, k_ref[rows, :], v_ref[rows, :], carry, None)
        carry = lax.fori_loop(0, qi, body, carry)
        rows = slice(qi * tq, (qi + 1) * tq)
        m, l, acc = step(qq, k_ref[rows, :], v_ref[rows, :], carry, causal)

        on = acc / l
        o = on[:tq] - lam * on[tq:]
        ms = jnp.mean(o * o, axis=-1, keepdims=True)
        o = o * lax.rsqrt(ms + EPS) * sg_ref[...] * (1.0 - LAM_INIT)
        o_ref[rows, :] = o.astype(BF16)


def _attn_call(packed, meta_packed, lq1, lk1, lq2, lk2, subln_g, *, batch, seq, tq):
    vec = pl.BlockSpec((1, HEAD_DIM), lambda b, h: (0, 0))
    return pl.pallas_call(
        functools.partial(_attn_kernel, tq=tq),
        grid=(batch, HEADS),
        in_specs=[
            pl.BlockSpec((seq, V_HEAD_DIM), lambda b, h: (b, h)),
            pl.BlockSpec((seq, V_HEAD_DIM), lambda b, h: (b, HEADS + h)),
            pl.BlockSpec((seq, V_HEAD_DIM), lambda b, h: (b, 2 * HEADS + h)),
            pl.BlockSpec((N_META, V_HEAD_DIM), lambda b, h: (0, h)),
            pl.BlockSpec((N_META, V_HEAD_DIM), lambda b, h: (0, HEADS + h)),
            vec, vec, vec, vec,
            pl.BlockSpec((1, V_HEAD_DIM), lambda b, h: (0, 0)),
        ],
        out_specs=pl.BlockSpec((seq, V_HEAD_DIM), lambda b, h: (b, h)),
        out_shape=jax.ShapeDtypeStruct((batch * seq, ATTN_WIDTH), BF16),
        compiler_params=pltpu.CompilerParams(
            dimension_semantics=("arbitrary", "arbitrary"), vmem_limit_bytes=VMEM_LIMIT),
        name="attn",
    )(packed, packed, packed, meta_packed, meta_packed, lq1, lk1, lq2, lk2, subln_g)


CONV_HALO = 32
CONV_ROWS = 64


def _conv_kernel(cur_ref, halo_ref, meta_ref, w_ref, b_ref, lg_ref, lb_ref, o_ref, buf_ref, y_ref):
    t = pl.program_id(1)
    ts = cur_ref.shape[0]

    @pl.when(t == 0)
    def _():
        buf_ref[0:CONV_HALO - N_META, :] = jnp.zeros((CONV_HALO - N_META, CONV_WIDTH), F32)
        buf_ref[CONV_HALO - N_META:CONV_HALO, :] = meta_ref[...].astype(F32)

    @pl.when(t > 0)
    def _():
        buf_ref[0:CONV_HALO, :] = halo_ref[...].astype(F32)

    buf_ref[CONV_HALO:CONV_HALO + ts, :] = cur_ref[...].astype(F32)
    first = CONV_HALO - (CONV_KERNEL - 1)

    for base in range(0, ts, CONV_ROWS):
        for c in range(CONV_WIDTH // LANES):
            lanes = slice(c * LANES, (c + 1) * LANES)
            acc = buf_ref[base + first:base + first + CONV_ROWS, lanes] * w_ref[0:1, lanes]
            for j in range(1, CONV_KERNEL):
                row0 = base + first + j
                acc = acc + buf_ref[row0:row0 + CONV_ROWS, lanes] * w_ref[j:j + 1, lanes]
            y_ref[base:base + CONV_ROWS, lanes] = acc + b_ref[:, lanes]

    y = y_ref[...]
    mu = jnp.mean(y, axis=-1, keepdims=True)
    yc = y - mu
    var = jnp.mean(yc * yc, axis=-1, keepdims=True)
    yn = yc * lax.rsqrt(var + EPS) * lg_ref[...] + lb_ref[...]
    o_ref[...] = (yn * _sigmoid(yn)).astype(BF16)


def _conv_call(packed, meta_packed, dw, db, lg, lb, *, batch, seq, ts):
    nt = seq // ts
    glu_blk = (3 * QK_WIDTH) // CONV_WIDTH
    meta_blk = (2 * QK_WIDTH) // CONV_WIDTH
    vec = pl.BlockSpec((1, CONV_WIDTH), lambda b, t: (0, 0))

    def halo_map(b, t):
        return (jnp.maximum((b * seq + t * ts) // CONV_HALO - 1, 0), glu_blk)

    return pl.pallas_call(
        _conv_kernel,
        grid=(batch, nt),
        in_specs=[
            pl.BlockSpec((ts, CONV_WIDTH), lambda b, t: (b * nt + t, glu_blk)),
            pl.BlockSpec((CONV_HALO, CONV_WIDTH), halo_map),
            pl.BlockSpec((N_META, CONV_WIDTH), lambda b, t: (0, meta_blk)),
            pl.BlockSpec((CONV_KERNEL, CONV_WIDTH), lambda b, t: (0, 0)),
            vec, vec, vec,
        ],
        out_specs=pl.BlockSpec((ts, CONV_WIDTH), lambda b, t: (b * nt + t, 0)),
        out_shape=jax.ShapeDtypeStruct((batch * seq, CONV_WIDTH), BF16),
        scratch_shapes=[pltpu.VMEM((CONV_HALO + ts, CONV_WIDTH), F32),
                        pltpu.VMEM((ts, CONV_WIDTH), F32)],
        compiler_params=pltpu.CompilerParams(
            dimension_semantics=("arbitrary", "arbitrary"), vmem_limit_bytes=VMEM_LIMIT),
        name="conv",
    )(packed, packed, meta_packed, dw, db, lg, lb)


def _merge_kernel(o_ref, c_ref, sga_ref, sgc_ref, x_ref, wa_ref, wc_ref, wo_ref, g2_ref,
                  h_ref, u2_ref):
    ya = jnp.dot(o_ref[...], wa_ref[...], preferred_element_type=F32)
    yc = jnp.dot(c_ref[...], wc_ref[...], preferred_element_type=F32)
    mix = (sga_ref[...].astype(F32) * ya + sgc_ref[...].astype(F32) * yc).astype(BF16)
    h = x_ref[...] + jnp.dot(mix, wo_ref[...], preferred_element_type=F32)
    h_ref[...] = h
    ms = jnp.mean(h * h, axis=-1, keepdims=True)
    u2_ref[...] = (h * lax.rsqrt(ms + EPS) * g2_ref[...]).astype(BF16)


def _merge_call(o, c, packed, x2, wa, wc, wo, g2, *, tm):
    m = x2.shape[0]
    gate_blk = (3 * QK_WIDTH + CONV_WIDTH) // D_MODEL

    def const(shape):
        return pl.BlockSpec(shape, lambda i: (0, 0), pipeline_mode=pl.Buffered(1))

    return pl.pallas_call(
        _merge_kernel,
        grid=(m // tm,),
        in_specs=[
            pl.BlockSpec((tm, ATTN_WIDTH), lambda i: (i, 0)),
            pl.BlockSpec((tm, CONV_WIDTH), lambda i: (i, 0)),
            pl.BlockSpec((tm, D_MODEL), lambda i: (i, gate_blk)),
            pl.BlockSpec((tm, D_MODEL), lambda i: (i, gate_blk + 1)),
            pl.BlockSpec((tm, D_MODEL), lambda i: (i, 0)),
            const((ATTN_WIDTH, D_MODEL)),
            const((CONV_WIDTH, D_MODEL)),
            const((D_MODEL, D_MODEL)),
            const((1, D_MODEL)),
        ],
        out_specs=[pl.BlockSpec((tm, D_MODEL), lambda i: (i, 0)),
                   pl.BlockSpec((tm, D_MODEL), lambda i: (i, 0))],
        out_shape=[jax.ShapeDtypeStruct((m, D_MODEL), F32),
                   jax.ShapeDtypeStruct((m, D_MODEL), BF16)],
        compiler_params=pltpu.CompilerParams(
            dimension_semantics=("arbitrary",), vmem_limit_bytes=VMEM_LIMIT),
        name="merge",
    )(o, c, packed, packed, x2, wa, wc, wo, g2)


def _mlp_kernel(u_ref, h_ref, wu_ref, wd_ref, o_ref, acc_ref):
    f = pl.program_id(1)

    @pl.when(f == 0)
    def _():
        acc_ref[...] = h_ref[...]

    z = jnp.maximum(jnp.dot(u_ref[...], wu_ref[...], preferred_element_type=F32), 0.0)
    z = (z * z).astype(BF16)
    acc_ref[...] += jnp.dot(z, wd_ref[...], preferred_element_type=F32)

    @pl.when(f == pl.num_programs(1) - 1)
    def _():
        o_ref[...] = acc_ref[...]


def _mlp_call(u2, h, wu, wd, *, tm, tf):
    m = u2.shape[0]
    return pl.pallas_call(
        _mlp_kernel,
        grid=(m // tm, D_FF // tf),
        in_specs=[
            pl.BlockSpec((tm, D_MODEL), lambda i, f: (i, 0)),
            pl.BlockSpec((tm, D_MODEL), lambda i, f: (i, 0)),
            pl.BlockSpec((D_MODEL, tf), lambda i, f: (0, f)),
            pl.BlockSpec((tf, D_MODEL), lambda i, f: (f, 0)),
        ],
        out_specs=pl.BlockSpec((tm, D_MODEL), lambda i, f: (i, 0)),
        out_shape=jax.ShapeDtypeStruct((m, D_MODEL), F32),
        scratch_shapes=[pltpu.VMEM((tm, D_MODEL), F32)],
        compiler_params=pltpu.CompilerParams(
            dimension_semantics=("arbitrary", "arbitrary"), vmem_limit_bytes=VMEM_LIMIT),
        name="mlp",
    )(u2, h, wu, wd)


def _rope_tables(length):
    pos = jnp.arange(length, dtype=F32)
    inv_freq = ROPE_THETA ** (-jnp.arange(0, ROT_DIM, 2, dtype=F32) / ROT_DIM)
    ang = pos[:, None] * inv_freq[None, :]
    half = ROT_DIM // 2
    lane = jnp.arange(LANES) % HEAD_DIM
    ang_l = ang[:, jnp.arange(LANES) % half]
    cos_t = jnp.where(lane < ROT_DIM, jnp.cos(ang_l), 1.0)
    sin_l = jnp.sin(ang_l)
    sa_t = jnp.where((lane >= half) & (lane < ROT_DIM), sin_l, 0.0)
    sb_t = jnp.where(lane < half, -sin_l, 0.0)
    return cos_t.astype(F32), sa_t.astype(F32), sb_t.astype(F32)


def kernel(x, meta, norm1_g, w_in, q_norm_g, k_norm_g, lambda_q1, lambda_k1, lambda_q2, lambda_k2,
           subln_g, w_attn_o, dw_kernel, dw_bias, conv_ln_g, conv_ln_b, w_conv_o, w_out, norm2_g,
           w_up, w_down):
    b, s, d = x.shape
    x2 = x.reshape(b * s, d)
    l = 0

    w_in_bf = w_in[l].astype(BF16)
    wa = w_attn_o[l].astype(BF16)
    wc = w_conv_o[l].astype(BF16)
    wo = w_out[l].astype(BF16)
    wu = w_up[l].astype(BF16)
    wd = w_down[l].astype(BF16)

    qkg = jnp.concatenate([jnp.tile(q_norm_g[l] * (HEAD_DIM ** -0.5), QK_WIDTH // HEAD_DIM),
                           jnp.tile(k_norm_g[l], QK_WIDTH // HEAD_DIM)])[None, :]
    cos_t, sa_t, sb_t = _rope_tables(N_META + s)
    grp = jnp.arange(LANES) // HEAD_DIM
    gsum = (grp[:, None] == grp[None, :]).astype(BF16)
    g1 = norm1_g[l][None, :]

    packed = _proj_call(x2, g1, w_in_bf, qkg, cos_t[N_META:], sa_t[N_META:], sb_t[N_META:], gsum,
                        tm=1024, j0=0, nj=N_PROJ_TILES)
    meta_packed = _proj_call(meta, g1, w_in_bf, qkg, cos_t[:N_META], sa_t[:N_META], sb_t[:N_META],
                             gsum, tm=N_META, j0=2, nj=6)

    o = _attn_call(packed, meta_packed, lambda_q1[l][None, :], lambda_k1[l][None, :],
                   lambda_q2[l][None, :], lambda_k2[l][None, :], subln_g[l][None, :],
                   batch=b, seq=s, tq=256)
    c = _conv_call(packed, meta_packed, dw_kernel[l], dw_bias[l][None, :], conv_ln_g[l][None, :],
                   conv_ln_b[l][None, :], batch=b, seq=s, ts=256)
    h, u2 = _merge_call(o, c, packed, x2, wa, wc, wo, norm2_g[l][None, :], tm=512)
    out = _mlp_call(u2, h, wu, wd, tm=512, tf=1024)
    return out.reshape(b, s, d)
```

```python
import functools
import math

import jax
import jax.numpy as jnp
from jax import lax
from jax.experimental import pallas as pl
from jax.experimental.pallas import tpu as pltpu

F32 = jnp.float32
BF16 = jnp.bfloat16

D_MODEL = 2048
N_META = 16
HEADS = 8
HEAD_DIM = 64
V_HEAD_DIM = 2 * HEAD_DIM
QK_WIDTH = HEADS * 2 * HEAD_DIM
ATTN_WIDTH = HEADS * V_HEAD_DIM
ROT_DIM = HEAD_DIM // 4
ROPE_THETA = 500000.0
CONV_WIDTH = D_MODEL // 2
CONV_KERNEL = 31
D_FF = 4 * D_MODEL
EPS = 1e-6
LAM_INIT = 0.8 - 0.6 * math.exp(-0.3 * 0)

LANES = 128
SUBLANES = 8
ATTN_ACC_PAD = 16
VMEM_LIMIT = 56 * 1024 * 1024

PROJ_TN = 512
PACKED_COLS = 3 * QK_WIDTH + CONV_WIDTH + 2 * D_MODEL
N_PROJ_TILES = PACKED_COLS // PROJ_TN
GLU_TILE0 = (3 * QK_WIDTH) // PROJ_TN
GATE_TILE0 = GLU_TILE0 + CONV_WIDTH // PROJ_TN
GLU_B_WTILE0 = (3 * QK_WIDTH + CONV_WIDTH) // PROJ_TN
N_QK_TILES = (2 * QK_WIDTH) // PROJ_TN


def _sigmoid(x):
    return 1.0 / (1.0 + jnp.exp(-x))


def _proj_kernel(x_ref, g_ref, w1_ref, w2_ref, qkg_ref, cos_ref, sa_ref, sb_ref, gsum_ref,
                 o_ref, u_ref, *, j0, row_chunk):
    j = pl.program_id(1) + j0
    tm = x_ref.shape[0]

    @pl.when(pl.program_id(1) == 0)
    def _():
        def body(r, carry):
            rows = pl.ds(pl.multiple_of(r * row_chunk, row_chunk), row_chunk)
            x = x_ref[rows, :]
            ms = jnp.mean(x * x, axis=-1, keepdims=True)
            u_ref[rows, :] = (x * lax.rsqrt(ms + EPS) * g_ref[...]).astype(BF16)
            return carry
        lax.fori_loop(0, tm // row_chunk, body, 0)

    @pl.when(j < N_QK_TILES)
    def _():
        y = jnp.dot(u_ref[...], w1_ref[...], preferred_element_type=F32)
        for c in range(PROJ_TN // LANES):
            lanes = slice(c * LANES, (c + 1) * LANES)
            yc = y[:, lanes]
            yy = yc * yc
            hi = yy.astype(BF16)
            lo = (yy - hi.astype(F32)).astype(BF16)
            ss = (jnp.dot(hi, gsum_ref[...], preferred_element_type=F32)
                  + jnp.dot(lo, gsum_ref[...], preferred_element_type=F32))
            yn = yc * lax.rsqrt(ss * (1.0 / HEAD_DIM) + EPS) * qkg_ref[:, lanes]
            out = (yn * cos_ref[...]
                   + pltpu.roll(yn, ROT_DIM // 2, 1) * sa_ref[...]
                   + pltpu.roll(yn, LANES - ROT_DIM // 2, 1) * sb_ref[...])
            o_ref[:, lanes] = out.astype(BF16)

    @pl.when((j >= N_QK_TILES) & (j < GLU_TILE0))
    def _():
        y = jnp.dot(u_ref[...], w1_ref[...], preferred_element_type=F32)
        o_ref[...] = y.astype(BF16)

    @pl.when((j >= GLU_TILE0) & (j < GATE_TILE0))
    def _():
        ya = jnp.dot(u_ref[...], w1_ref[...], preferred_element_type=F32)
        yb = jnp.dot(u_ref[...], w2_ref[...], preferred_element_type=F32)
        o_ref[...] = (ya * _sigmoid(yb)).astype(BF16)

    @pl.when(j >= GATE_TILE0)
    def _():
        y = jnp.dot(u_ref[...], w1_ref[...], preferred_element_type=F32)
        o_ref[...] = _sigmoid(y).astype(BF16)


def _proj_call(x2, g, w_in_bf, qkg, cos_t, sa_t, sb_t, gsum, *, tm, j0, nj):
    m = x2.shape[0]
    npb = cos_t.shape[0] // tm

    def w1_map(i, j):
        jj = j + j0
        return (0, jnp.where(jj < GATE_TILE0, jj, jj + 2))

    def w2_map(i, j):
        return (0, jnp.clip(j + j0 + 2, GLU_B_WTILE0, GLU_B_WTILE0 + 1))

    def qkg_map(i, j):
        return (0, jnp.minimum(j + j0, N_QK_TILES - 1))

    kern = functools.partial(_proj_kernel, j0=j0, row_chunk=min(tm, 64))
    return pl.pallas_call(
        kern,
        grid=(m // tm, nj),
        in_specs=[
            pl.BlockSpec((tm, D_MODEL), lambda i, j: (i, 0)),
            pl.BlockSpec((1, D_MODEL), lambda i, j: (0, 0)),
            pl.BlockSpec((D_MODEL, PROJ_TN), w1_map),
            pl.BlockSpec((D_MODEL, PROJ_TN), w2_map),
            pl.BlockSpec((1, PROJ_TN), qkg_map),
            pl.BlockSpec((tm, LANES), lambda i, j: (i % npb, 0)),
            pl.BlockSpec((tm, LANES), lambda i, j: (i % npb, 0)),
            pl.BlockSpec((tm, LANES), lambda i, j: (i % npb, 0)),
            pl.BlockSpec((LANES, LANES), lambda i, j: (0, 0)),
        ],
        out_specs=pl.BlockSpec((tm, PROJ_TN), lambda i, j: (i, j)),
        out_shape=jax.ShapeDtypeStruct((m, nj * PROJ_TN), BF16),
        scratch_shapes=[pltpu.VMEM((tm, D_MODEL), BF16)],
        compiler_params=pltpu.CompilerParams(
            dimension_semantics=("arbitrary", "arbitrary"), vmem_limit_bytes=VMEM_LIMIT),
        name="proj",
    )(x2, g, w_in_bf, w_in_bf, qkg, cos_t, sa_t, sb_t, gsum)


def _attn_kernel(q_ref, k_ref, v_ref, km_ref, vmt_ref, lq1_ref, lk1_ref, lq2_ref, lk2_ref, sg_ref,
                 o_ref, qt_ref, vt_ref, qq_ref, st_ref, acc_ref, m_ref, *, tq, heads):
    seq = q_ref.shape[0]
    nq = seq // tq
    lam = (jnp.exp(jnp.sum(lq1_ref[...] * lk1_ref[...], axis=-1, keepdims=True))
           - jnp.exp(jnp.sum(lq2_ref[...] * lk2_ref[...], axis=-1, keepdims=True))
           + LAM_INIT)
    k_io = lax.broadcasted_iota(jnp.int32, (tq, 2 * tq), 0)
    q_io = lax.broadcasted_iota(jnp.int32, (tq, 2 * tq), 1)
    causal = k_io <= jnp.where(q_io >= tq, q_io - tq, q_io)
    head_lanes = [slice(g * V_HEAD_DIM, (g + 1) * V_HEAD_DIM) for g in range(heads)]
    pad_io = lax.broadcasted_iota(jnp.int32, (ATTN_ACC_PAD, tq), 0)
    ones_row = jnp.where(pad_io == 0, 1.0, 0.0).astype(BF16)
    meta_pad = jnp.zeros((LANES - N_META, 2 * tq), BF16)

    for g in range(heads):
        for c in range(nq):
            rows = slice(c * tq, (c + 1) * tq)
            qt_ref[g, c] = q_ref[rows, head_lanes[g]].astype(F32).T.astype(BF16)
            vt_ref[g, c, :V_HEAD_DIM] = v_ref[rows, head_lanes[g]].astype(F32).T.astype(BF16)
            vt_ref[g, c, V_HEAD_DIM:] = ones_row

    def scores(g, kb, slot):
        st_ref[g, slot] = jnp.dot(kb, qq_ref[g], preferred_element_type=F32)

    def update(g, slot, vt, mask):
        st = st_ref[g, slot]
        if mask is not None:
            st = jnp.where(mask, st, -jnp.inf)
        m_old = m_ref[g]
        m_new = jnp.maximum(m_old, jnp.max(st, axis=0, keepdims=True))
        alpha = jnp.exp2(m_old - m_new)
        p = jnp.exp2(st - m_new).astype(BF16)
        acc_ref[g] = alpha * acc_ref[g] + jnp.dot(vt, p, preferred_element_type=F32)
        m_ref[g] = m_new

    zeros = jnp.zeros((HEAD_DIM, tq), BF16)

    def full_block(t, slot):
        nxt = pl.ds(pl.multiple_of((t + 1) * tq, tq), tq)
        for g in range(heads):
            scores(g, k_ref[nxt, head_lanes[g]], 1 - slot)
        for g in range(heads):
            update(g, slot, vt_ref[g, t], None)

    def query_block(qi, carry):
        for g in range(heads):
            qt = qt_ref[g, qi]
            qq_ref[g, :, :tq] = jnp.concatenate([qt[:HEAD_DIM], zeros], axis=0)
            qq_ref[g, :, tq:] = jnp.concatenate([zeros, qt[HEAD_DIM:]], axis=0)
            scores(g, k_ref[0:tq, head_lanes[g]], 0)

            st = jnp.dot(km_ref[:, head_lanes[g]], qq_ref[g], preferred_element_type=F32)
            m = jnp.max(st, axis=0, keepdims=True)
            p = jnp.concatenate([jnp.exp2(st - m).astype(BF16), meta_pad], axis=0)
            m_ref[g] = m
            acc_ref[g] = jnp.dot(vmt_ref[g], p, preferred_element_type=F32)

        def pair(tt, c):
            full_block(2 * tt, 0)
            full_block(2 * tt + 1, 1)
            return c
        lax.fori_loop(0, qi // 2, pair, 0)
        odd = lax.rem(qi, 2)

        def last_full(_, c):
            full_block(qi - 1, 0)
            return c
        lax.fori_loop(0, odd, last_full, 0)

        @pl.when(odd == 0)
        def _():
            for g in range(heads):
                update(g, 0, vt_ref[g, qi], causal)

        @pl.when(odd == 1)
        def _():
            for g in range(heads):
                update(g, 1, vt_ref[g, qi], causal)

        rows = pl.ds(pl.multiple_of(qi * tq, tq), tq)
        for g in range(heads):
            acc = acc_ref[g]
            on = acc[:V_HEAD_DIM] * (1.0 / acc[V_HEAD_DIM:V_HEAD_DIM + 1])
            ot = on[:, :tq] - lam * on[:, tq:]
            ms = jnp.mean(ot * ot, axis=0, keepdims=True)
            o = (ot * lax.rsqrt(ms + EPS)).T * sg_ref[...] * (1.0 - LAM_INIT)
            o_ref[rows, head_lanes[g]] = o.astype(BF16)
        return carry
    lax.fori_loop(0, nq, query_block, 0)


def _attn_call(packed, meta_packed, lq1, lk1, lq2, lk2, subln_g, *, batch, seq, tq, heads):
    vec = pl.BlockSpec((1, HEAD_DIM), lambda b, h: (0, 0))
    nq = seq // tq
    width = heads * V_HEAD_DIM
    ng = HEADS // heads
    return pl.pallas_call(
        functools.partial(_attn_kernel, tq=tq, heads=heads),
        scratch_shapes=[
            pltpu.VMEM((heads, nq, V_HEAD_DIM, tq), BF16),
            pltpu.VMEM((heads, nq, V_HEAD_DIM + ATTN_ACC_PAD, tq), BF16),
            pltpu.VMEM((heads, V_HEAD_DIM, 2 * tq), BF16),
            pltpu.VMEM((heads, 2, tq, 2 * tq), F32),
            pltpu.VMEM((heads, V_HEAD_DIM + ATTN_ACC_PAD, 2 * tq), F32),
            pltpu.VMEM((heads, 1, 2 * tq), F32),
        ],
        grid=(batch, ng),
        in_specs=[
            pl.BlockSpec((seq, width), lambda b, h: (b, h)),
            pl.BlockSpec((seq, width), lambda b, h: (b, ng + h)),
            pl.BlockSpec((seq, width), lambda b, h: (b, 2 * ng + h)),
            pl.BlockSpec((N_META, width), lambda b, h: (0, h)),
            pl.BlockSpec((heads, V_HEAD_DIM + ATTN_ACC_PAD, LANES), lambda b, h: (h, 0, 0)),
            vec, vec, vec, vec,
            pl.BlockSpec((1, V_HEAD_DIM), lambda b, h: (0, 0)),
        ],
        out_specs=pl.BlockSpec((seq, width), lambda b, h: (b, h)),
        out_shape=jax.ShapeDtypeStruct((batch * seq, ATTN_WIDTH), BF16),
        compiler_params=pltpu.CompilerParams(
            dimension_semantics=("arbitrary", "arbitrary"), vmem_limit_bytes=VMEM_LIMIT),
        name="attn",
    )(packed, packed, packed, meta_packed, _meta_values_transposed(meta_packed),
      lq1, lk1, lq2, lk2, subln_g)


def _meta_values_transposed(meta_packed):
    vm = meta_packed[:, QK_WIDTH:QK_WIDTH + ATTN_WIDTH].reshape(N_META, HEADS, V_HEAD_DIM)
    vmt = jnp.pad(vm.transpose(1, 2, 0), ((0, 0), (0, 0), (0, LANES - N_META)))
    ones = (jnp.arange(LANES) < N_META).astype(BF16)
    den = jnp.zeros((HEADS, ATTN_ACC_PAD, LANES), BF16).at[:, 0, :].set(ones)
    return jnp.concatenate([vmt, den], axis=1)


CONV_HALO = 32
CONV_ROWS = 64


def _conv_kernel(cur_ref, halo_ref, meta_ref, w_ref, b_ref, lg_ref, lb_ref, o_ref,
                 buf_ref, sh_ref, y_ref):
    t = pl.program_id(1)
    ts = cur_ref.shape[0]

    @pl.when(t == 0)
    def _():
        buf_ref[0:CONV_HALO - N_META, :] = jnp.zeros((CONV_HALO - N_META, CONV_WIDTH), F32)
        buf_ref[CONV_HALO - N_META:CONV_HALO, :] = meta_ref[...].astype(F32)

    @pl.when(t > 0)
    def _():
        buf_ref[0:CONV_HALO, :] = halo_ref[...].astype(F32)

    buf_ref[CONV_HALO:CONV_HALO + ts, :] = cur_ref[...].astype(F32)
    first = CONV_HALO - (CONV_KERNEL - 1)

    n_sh = ts + CONV_HALO - SUBLANES
    for s in range(1, SUBLANES):
        sh_ref[s - 1, 0:n_sh, :] = buf_ref[s:s + n_sh, :]

    def tap(base, j, lanes):
        shift, row0 = (first + j) % SUBLANES, base + (first + j) // SUBLANES * SUBLANES
        src = buf_ref if shift == 0 else sh_ref.at[shift - 1]
        return src[row0:row0 + CONV_ROWS, lanes] * w_ref[j:j + 1, lanes]

    for base in range(0, ts, CONV_ROWS):
        for c in range(CONV_WIDTH // LANES):
            lanes = slice(c * LANES, (c + 1) * LANES)
            acc = tap(base, 0, lanes)
            for j in range(1, CONV_KERNEL):
                acc = acc + tap(base, j, lanes)
            y_ref[base:base + CONV_ROWS, lanes] = acc + b_ref[:, lanes]

    y = y_ref[...]
    mu = jnp.mean(y, axis=-1, keepdims=True)
    yc = y - mu
    var = jnp.mean(yc * yc, axis=-1, keepdims=True)
    yn = yc * lax.rsqrt(var + EPS) * lg_ref[...] + lb_ref[...]
    o_ref[...] = (yn * _sigmoid(yn)).astype(BF16)


def _conv_call(packed, meta_packed, dw, db, lg, lb, *, batch, seq, ts):
    nt = seq // ts
    glu_blk = (3 * QK_WIDTH) // CONV_WIDTH
    meta_blk = (2 * QK_WIDTH) // CONV_WIDTH
    vec = pl.BlockSpec((1, CONV_WIDTH), lambda b, t: (0, 0))

    def halo_map(b, t):
        return (jnp.maximum((b * seq + t * ts) // CONV_HALO - 1, 0), glu_blk)

    return pl.pallas_call(
        _conv_kernel,
        grid=(batch, nt),
        in_specs=[
            pl.BlockSpec((ts, CONV_WIDTH), lambda b, t: (b * nt + t, glu_blk)),
            pl.BlockSpec((CONV_HALO, CONV_WIDTH), halo_map),
            pl.BlockSpec((N_META, CONV_WIDTH), lambda b, t: (0, meta_blk)),
            pl.BlockSpec((CONV_KERNEL, CONV_WIDTH), lambda b, t: (0, 0)),
            vec, vec, vec,
        ],
        out_specs=pl.BlockSpec((ts, CONV_WIDTH), lambda b, t: (b * nt + t, 0)),
        out_shape=jax.ShapeDtypeStruct((batch * seq, CONV_WIDTH), BF16),
        scratch_shapes=[pltpu.VMEM((CONV_HALO + ts, CONV_WIDTH), F32),
                        pltpu.VMEM((SUBLANES - 1, CONV_HALO + ts, CONV_WIDTH), F32),
                        pltpu.VMEM((ts, CONV_WIDTH), F32)],
        compiler_params=pltpu.CompilerParams(
            dimension_semantics=("arbitrary", "arbitrary"), vmem_limit_bytes=VMEM_LIMIT),
        name="conv",
    )(packed, packed, meta_packed, dw, db, lg, lb)


def _merge_kernel(o_ref, c_ref, sga_ref, sgc_ref, x_ref, wa_ref, wc_ref, wo_ref, g2_ref,
                  h_ref, u2_ref):
    ya = jnp.dot(o_ref[...], wa_ref[...], preferred_element_type=F32)
    yc = jnp.dot(c_ref[...], wc_ref[...], preferred_element_type=F32)
    mix = (sga_ref[...].astype(F32) * ya + sgc_ref[...].astype(F32) * yc).astype(BF16)
    h = x_ref[...] + jnp.dot(mix, wo_ref[...], preferred_element_type=F32)
    h_ref[...] = h
    ms = jnp.mean(h * h, axis=-1, keepdims=True)
    u2_ref[...] = (h * lax.rsqrt(ms + EPS) * g2_ref[...]).astype(BF16)


def _merge_call(o, c, packed, x2, wa, wc, wo, g2, *, tm):
    m = x2.shape[0]
    gate_blk = (3 * QK_WIDTH + CONV_WIDTH) // D_MODEL

    def const(shape):
        return pl.BlockSpec(shape, lambda i: (0, 0), pipeline_mode=pl.Buffered(1))

    return pl.pallas_call(
        _merge_kernel,
        grid=(m // tm,),
        in_specs=[
            pl.BlockSpec((tm, ATTN_WIDTH), lambda i: (i, 0)),
            pl.BlockSpec((tm, CONV_WIDTH), lambda i: (i, 0)),
            pl.BlockSpec((tm, D_MODEL), lambda i: (i, gate_blk)),
            pl.BlockSpec((tm, D_MODEL), lambda i: (i, gate_blk + 1)),
            pl.BlockSpec((tm, D_MODEL), lambda i: (i, 0)),
            const((ATTN_WIDTH, D_MODEL)),
            const((CONV_WIDTH, D_MODEL)),
            const((D_MODEL, D_MODEL)),
            const((1, D_MODEL)),
        ],
        out_specs=[pl.BlockSpec((tm, D_MODEL), lambda i: (i, 0)),
                   pl.BlockSpec((tm, D_MODEL), lambda i: (i, 0))],
        out_shape=[jax.ShapeDtypeStruct((m, D_MODEL), F32),
                   jax.ShapeDtypeStruct((m, D_MODEL), BF16)],
        compiler_params=pltpu.CompilerParams(
            dimension_semantics=("arbitrary",), vmem_limit_bytes=VMEM_LIMIT),
        name="merge",
    )(o, c, packed, packed, x2, wa, wc, wo, g2)


def _mlp_kernel(u_ref, h_ref, wu_ref, wd_ref, o_ref, acc_ref):
    f = pl.program_id(1)

    @pl.when(f == 0)
    def _():
        acc_ref[...] = h_ref[...]

    z = jnp.maximum(jnp.dot(u_ref[...], wu_ref[...], preferred_element_type=F32), 0.0)
    z = (z * z).astype(BF16)
    acc_ref[...] += jnp.dot(z, wd_ref[...], preferred_element_type=F32)

    @pl.when(f == pl.num_programs(1) - 1)
    def _():
        o_ref[...] = acc_ref[...]


def _mlp_call(u2, h, wu, wd, *, tm, tf):
    m = u2.shape[0]
    return pl.pallas_call(
        _mlp_kernel,
        grid=(m // tm, D_FF // tf),
        in_specs=[
            pl.BlockSpec((tm, D_MODEL), lambda i, f: (i, 0)),
            pl.BlockSpec((tm, D_MODEL), lambda i, f: (i, 0)),
            pl.BlockSpec((D_MODEL, tf), lambda i, f: (0, f)),
            pl.BlockSpec((tf, D_MODEL), lambda i, f: (f, 0)),
        ],
        out_specs=pl.BlockSpec((tm, D_MODEL), lambda i, f: (i, 0)),
        out_shape=jax.ShapeDtypeStruct((m, D_MODEL), F32),
        scratch_shapes=[pltpu.VMEM((tm, D_MODEL), F32)],
        compiler_params=pltpu.CompilerParams(
            dimension_semantics=("arbitrary", "arbitrary"), vmem_limit_bytes=VMEM_LIMIT),
        name="mlp",
    )(u2, h, wu, wd)


def _rope_tables(length):
    pos = jnp.arange(length, dtype=F32)
    inv_freq = ROPE_THETA ** (-jnp.arange(0, ROT_DIM, 2, dtype=F32) / ROT_DIM)
    ang = pos[:, None] * inv_freq[None, :]
    half = ROT_DIM // 2
    lane = jnp.arange(LANES) % HEAD_DIM
    ang_l = ang[:, jnp.arange(LANES) % half]
    cos_t = jnp.where(lane < ROT_DIM, jnp.cos(ang_l), 1.0)
    sin_l = jnp.sin(ang_l)
    sa_t = jnp.where((lane >= half) & (lane < ROT_DIM), sin_l, 0.0)
    sb_t = jnp.where(lane < half, -sin_l, 0.0)
    return cos_t.astype(F32), sa_t.astype(F32), sb_t.astype(F32)


def kernel(x, meta, norm1_g, w_in, q_norm_g, k_norm_g, lambda_q1, lambda_k1, lambda_q2, lambda_k2,
           subln_g, w_attn_o, dw_kernel, dw_bias, conv_ln_g, conv_ln_b, w_conv_o, w_out, norm2_g,
           w_up, w_down):
    b, s, d = x.shape
    x2 = x.reshape(b * s, d)
    l = 0

    w_in_bf = w_in[l].astype(BF16)
    wa = w_attn_o[l].astype(BF16)
    wc = w_conv_o[l].astype(BF16)
    wo = w_out[l].astype(BF16)
    wu = w_up[l].astype(BF16)
    wd = w_down[l].astype(BF16)

    qkg = jnp.concatenate([jnp.tile(q_norm_g[l] * (HEAD_DIM ** -0.5 * math.log2(math.e)),
                                    QK_WIDTH // HEAD_DIM),
                           jnp.tile(k_norm_g[l], QK_WIDTH // HEAD_DIM)])[None, :]
    cos_t, sa_t, sb_t = _rope_tables(N_META + s)
    grp = jnp.arange(LANES) // HEAD_DIM
    gsum = (grp[:, None] == grp[None, :]).astype(BF16)
    g1 = norm1_g[l][None, :]

    packed = _proj_call(x2, g1, w_in_bf, qkg, cos_t[N_META:], sa_t[N_META:], sb_t[N_META:], gsum,
                        tm=1024, j0=0, nj=N_PROJ_TILES)
    meta_packed = _proj_call(meta, g1, w_in_bf, qkg, cos_t[:N_META], sa_t[:N_META], sb_t[:N_META],
                             gsum, tm=N_META, j0=2, nj=6)

    o = _attn_call(packed, meta_packed, lambda_q1[l][None, :], lambda_k1[l][None, :],
                   lambda_q2[l][None, :], lambda_k2[l][None, :], subln_g[l][None, :],
                   batch=b, seq=s, tq=256, heads=2)
    c = _conv_call(packed, meta_packed, dw_kernel[l], dw_bias[l][None, :], conv_ln_g[l][None, :],
                   conv_ln_b[l][None, :], batch=b, seq=s, ts=256)
    h, u2 = _merge_call(o, c, packed, x2, wa, wc, wo, norm2_g[l][None, :], tm=512)
    out = _mlp_call(u2, h, wu, wd, tm=512, tf=1024)
    return out.reshape(b, s, d)
```

```python
import functools
import math

import jax
import jax.numpy as jnp
from jax import lax
from jax.experimental import pallas as pl
from jax.experimental.pallas import tpu as pltpu

F32 = jnp.float32
BF16 = jnp.bfloat16

D_MODEL = 2048
N_META = 16
HEADS = 8
HEAD_DIM = 64
V_HEAD_DIM = 2 * HEAD_DIM
QK_WIDTH = HEADS * 2 * HEAD_DIM
ATTN_WIDTH = HEADS * V_HEAD_DIM
ROT_DIM = HEAD_DIM // 4
ROPE_THETA = 500000.0
CONV_WIDTH = D_MODEL // 2
CONV_KERNEL = 31
D_FF = 4 * D_MODEL
EPS = 1e-6
LAM_INIT = 0.8 - 0.6 * math.exp(-0.3 * 0)

LANES = 128
SUBLANES = 8
ATTN_ACC_PAD = 16
VMEM_LIMIT = 56 * 1024 * 1024

PROJ_TN = 1024
PACKED_COLS = 3 * QK_WIDTH + CONV_WIDTH + 2 * D_MODEL
N_PROJ_TILES = PACKED_COLS // PROJ_TN
N_GLU_TILES = CONV_WIDTH // PROJ_TN
GLU_TILE0 = (3 * QK_WIDTH) // PROJ_TN
GATE_TILE0 = GLU_TILE0 + N_GLU_TILES
GLU_B_WTILE0 = (3 * QK_WIDTH + CONV_WIDTH) // PROJ_TN
N_QK_TILES = (2 * QK_WIDTH) // PROJ_TN
META_TILE0 = QK_WIDTH // PROJ_TN
N_META_TILES = GATE_TILE0 - META_TILE0


def _sigmoid(x):
    return 1.0 / (1.0 + jnp.exp(-x))


def _proj_kernel(x_ref, g_ref, w1_ref, w2_ref, qkg_ref, cos_ref, sa_ref, sb_ref, gsum_ref,
                 o_ref, u_ref, *, j0, row_chunk):
    j = pl.program_id(1) + j0
    tm = x_ref.shape[0]

    @pl.when(pl.program_id(1) == 0)
    def _():
        def body(r, carry):
            rows = pl.ds(pl.multiple_of(r * row_chunk, row_chunk), row_chunk)
            x = x_ref[rows, :]
            ms = jnp.mean(x * x, axis=-1, keepdims=True)
            u_ref[rows, :] = (x * lax.rsqrt(ms + EPS) * g_ref[...]).astype(BF16)
            return carry
        lax.fori_loop(0, tm // row_chunk, body, 0)

    @pl.when(j < N_QK_TILES)
    def _():
        y = jnp.dot(u_ref[...], w1_ref[...], preferred_element_type=F32)
        for c in range(PROJ_TN // LANES):
            lanes = slice(c * LANES, (c + 1) * LANES)
            yc = y[:, lanes]
            yy = yc * yc
            hi = yy.astype(BF16)
            lo = (yy - hi.astype(F32)).astype(BF16)
            ss = (jnp.dot(hi, gsum_ref[...], preferred_element_type=F32)
                  + jnp.dot(lo, gsum_ref[...], preferred_element_type=F32))
            yn = yc * lax.rsqrt(ss * (1.0 / HEAD_DIM) + EPS) * qkg_ref[:, lanes]
            out = (yn * cos_ref[...]
                   + pltpu.roll(yn, ROT_DIM // 2, 1) * sa_ref[...]
                   + pltpu.roll(yn, LANES - ROT_DIM // 2, 1) * sb_ref[...])
            o_ref[:, lanes] = out.astype(BF16)

    @pl.when((j >= N_QK_TILES) & (j < GLU_TILE0))
    def _():
        y = jnp.dot(u_ref[...], w1_ref[...], preferred_element_type=F32)
        o_ref[...] = y.astype(BF16)

    @pl.when((j >= GLU_TILE0) & (j < GATE_TILE0))
    def _():
        ya = jnp.dot(u_ref[...], w1_ref[...], preferred_element_type=F32)
        yb = jnp.dot(u_ref[...], w2_ref[...], preferred_element_type=F32)
        o_ref[...] = (ya * _sigmoid(yb)).astype(BF16)

    @pl.when(j >= GATE_TILE0)
    def _():
        y = jnp.dot(u_ref[...], w1_ref[...], preferred_element_type=F32)
        o_ref[...] = _sigmoid(y).astype(BF16)


def _proj_call(x2, g, w_in_bf, qkg, cos_t, sa_t, sb_t, gsum, *, tm, j0, nj):
    m = x2.shape[0]
    npb = cos_t.shape[0] // tm

    def w1_map(i, j):
        jj = j + j0
        return (jnp.where(jj < GATE_TILE0, jj, jj + N_GLU_TILES), 0, 0)

    def w2_map(i, j):
        return (jnp.clip(j + j0 + N_GLU_TILES, GLU_B_WTILE0, GLU_B_WTILE0 + N_GLU_TILES - 1), 0, 0)

    def qkg_map(i, j):
        return (0, jnp.minimum(j + j0, N_QK_TILES - 1))

    kern = functools.partial(_proj_kernel, j0=j0, row_chunk=min(tm, 64))
    return pl.pallas_call(
        kern,
        grid=(m // tm, nj),
        in_specs=[
            pl.BlockSpec((tm, D_MODEL), lambda i, j: (i, 0)),
            pl.BlockSpec((1, D_MODEL), lambda i, j: (0, 0)),
            pl.BlockSpec((None, D_MODEL, PROJ_TN), w1_map),
            pl.BlockSpec((None, D_MODEL, PROJ_TN), w2_map),
            pl.BlockSpec((1, PROJ_TN), qkg_map),
            pl.BlockSpec((tm, LANES), lambda i, j: (i % npb, 0)),
            pl.BlockSpec((tm, LANES), lambda i, j: (i % npb, 0)),
            pl.BlockSpec((tm, LANES), lambda i, j: (i % npb, 0)),
            pl.BlockSpec((LANES, LANES), lambda i, j: (0, 0)),
        ],
        out_specs=pl.BlockSpec((tm, PROJ_TN), lambda i, j: (i, j)),
        out_shape=jax.ShapeDtypeStruct((m, nj * PROJ_TN), BF16),
        scratch_shapes=[pltpu.VMEM((tm, D_MODEL), BF16)],
        compiler_params=pltpu.CompilerParams(
            dimension_semantics=("arbitrary", "arbitrary"), vmem_limit_bytes=VMEM_LIMIT),
        name="proj",
    )(x2, g, w_in_bf, w_in_bf, qkg, cos_t, sa_t, sb_t, gsum)


def _attn_kernel(q_ref, k_ref, v_ref, km_ref, vmt_ref, lq1_ref, lk1_ref, lq2_ref, lk2_ref, sg_ref,
                 o_ref, qt_ref, vt_ref, qq_ref, st_ref, acc_ref, m_ref, *, tq, heads):
    seq = q_ref.shape[0]
    nq = seq // tq
    lam = (jnp.exp(jnp.sum(lq1_ref[...] * lk1_ref[...], axis=-1, keepdims=True))
           - jnp.exp(jnp.sum(lq2_ref[...] * lk2_ref[...], axis=-1, keepdims=True))
           + LAM_INIT)
    k_io = lax.broadcasted_iota(jnp.int32, (tq, 2 * tq), 0)
    q_io = lax.broadcasted_iota(jnp.int32, (tq, 2 * tq), 1)
    causal = k_io <= jnp.where(q_io >= tq, q_io - tq, q_io)
    head_lanes = [slice(g * V_HEAD_DIM, (g + 1) * V_HEAD_DIM) for g in range(heads)]
    pad_io = lax.broadcasted_iota(jnp.int32, (ATTN_ACC_PAD, tq), 0)
    ones_row = jnp.where(pad_io == 0, 1.0, 0.0).astype(BF16)
    meta_pad = jnp.zeros((LANES - N_META, 2 * tq), BF16)

    for g in range(heads):
        for c in range(nq):
            rows = slice(c * tq, (c + 1) * tq)
            qt_ref[g, c] = q_ref[rows, head_lanes[g]].astype(F32).T.astype(BF16)
            vt_ref[g, c, :V_HEAD_DIM] = v_ref[rows, head_lanes[g]].astype(F32).T.astype(BF16)
            vt_ref[g, c, V_HEAD_DIM:] = ones_row

    def scores(g, kb, slot):
        st_ref[g, slot] = jnp.dot(kb, qq_ref[g], preferred_element_type=F32)

    def update(g, slot, vt, mask):
        st = st_ref[g, slot]
        if mask is not None:
            st = jnp.where(mask, st, -jnp.inf)
        m_old = m_ref[g]
        m_new = jnp.maximum(m_old, jnp.max(st, axis=0, keepdims=True))
        alpha = jnp.exp2(m_old - m_new)
        p = jnp.exp2(st - m_new).astype(BF16)
        acc_ref[g] = alpha * acc_ref[g] + jnp.dot(vt, p, preferred_element_type=F32)
        m_ref[g] = m_new

    zeros = jnp.zeros((HEAD_DIM, tq), BF16)

    def full_block(t, slot):
        nxt = pl.ds(pl.multiple_of((t + 1) * tq, tq), tq)
        for g in range(heads):
            scores(g, k_ref[nxt, head_lanes[g]], 1 - slot)
        for g in range(heads):
            update(g, slot, vt_ref[g, t], None)

    def query_block(qi, carry):
        for g in range(heads):
            qt = qt_ref[g, qi]
            qq_ref[g, :, :tq] = jnp.concatenate([qt[:HEAD_DIM], zeros], axis=0)
            qq_ref[g, :, tq:] = jnp.concatenate([zeros, qt[HEAD_DIM:]], axis=0)
            scores(g, k_ref[0:tq, head_lanes[g]], 0)

            st = jnp.dot(km_ref[:, head_lanes[g]], qq_ref[g], preferred_element_type=F32)
            m = jnp.max(st, axis=0, keepdims=True)
            p = jnp.concatenate([jnp.exp2(st - m).astype(BF16), meta_pad], axis=0)
            m_ref[g] = m
            acc_ref[g] = jnp.dot(vmt_ref[g], p, preferred_element_type=F32)

        def pair(tt, c):
            full_block(2 * tt, 0)
            full_block(2 * tt + 1, 1)
            return c
        lax.fori_loop(0, qi // 2, pair, 0)
        odd = lax.rem(qi, 2)

        def last_full(_, c):
            full_block(qi - 1, 0)
            return c
        lax.fori_loop(0, odd, last_full, 0)

        @pl.when(odd == 0)
        def _():
            for g in range(heads):
                update(g, 0, vt_ref[g, qi], causal)

        @pl.when(odd == 1)
        def _():
            for g in range(heads):
                update(g, 1, vt_ref[g, qi], causal)

        rows = pl.ds(pl.multiple_of(qi * tq, tq), tq)
        for g in range(heads):
            acc = acc_ref[g]
            on = acc[:V_HEAD_DIM] * (1.0 / acc[V_HEAD_DIM:V_HEAD_DIM + 1])
            ot = on[:, :tq] - lam * on[:, tq:]
            ms = jnp.mean(ot * ot, axis=0, keepdims=True)
            o = (ot * lax.rsqrt(ms + EPS)).T * sg_ref[...] * (1.0 - LAM_INIT)
            o_ref[rows, head_lanes[g]] = o.astype(BF16)
        return carry
    lax.fori_loop(0, nq, query_block, 0)


def _attn_call(packed, meta_packed, lq1, lk1, lq2, lk2, subln_g, *, batch, seq, tq, heads):
    vec = pl.BlockSpec((1, HEAD_DIM), lambda b, h: (0, 0))
    nq = seq // tq
    width = heads * V_HEAD_DIM
    ng = HEADS // heads
    return pl.pallas_call(
        functools.partial(_attn_kernel, tq=tq, heads=heads),
        scratch_shapes=[
            pltpu.VMEM((heads, nq, V_HEAD_DIM, tq), BF16),
            pltpu.VMEM((heads, nq, V_HEAD_DIM + ATTN_ACC_PAD, tq), BF16),
            pltpu.VMEM((heads, V_HEAD_DIM, 2 * tq), BF16),
            pltpu.VMEM((heads, 2, tq, 2 * tq), F32),
            pltpu.VMEM((heads, V_HEAD_DIM + ATTN_ACC_PAD, 2 * tq), F32),
            pltpu.VMEM((heads, 1, 2 * tq), F32),
        ],
        grid=(batch, ng),
        in_specs=[
            pl.BlockSpec((seq, width), lambda b, h: (b, h)),
            pl.BlockSpec((seq, width), lambda b, h: (b, ng + h)),
            pl.BlockSpec((seq, width), lambda b, h: (b, 2 * ng + h)),
            pl.BlockSpec((N_META, width), lambda b, h: (0, h)),
            pl.BlockSpec((heads, V_HEAD_DIM + ATTN_ACC_PAD, LANES), lambda b, h: (h, 0, 0)),
            vec, vec, vec, vec,
            pl.BlockSpec((1, V_HEAD_DIM), lambda b, h: (0, 0)),
        ],
        out_specs=pl.BlockSpec((seq, width), lambda b, h: (b, h)),
        out_shape=jax.ShapeDtypeStruct((batch * seq, ATTN_WIDTH), BF16),
        compiler_params=pltpu.CompilerParams(
            dimension_semantics=("arbitrary", "arbitrary"), vmem_limit_bytes=VMEM_LIMIT),
        name="attn",
    )(packed, packed, packed, meta_packed, _meta_values_transposed(meta_packed),
      lq1, lk1, lq2, lk2, subln_g)


def _meta_values_transposed(meta_packed):
    vm = meta_packed[:, QK_WIDTH:QK_WIDTH + ATTN_WIDTH].reshape(N_META, HEADS, V_HEAD_DIM)
    vmt = jnp.pad(vm.transpose(1, 2, 0), ((0, 0), (0, 0), (0, LANES - N_META)))
    ones = (jnp.arange(LANES) < N_META).astype(BF16)
    den = jnp.zeros((HEADS, ATTN_ACC_PAD, LANES), BF16).at[:, 0, :].set(ones)
    return jnp.concatenate([vmt, den], axis=1)


CONV_HALO = 32
CONV_ROWS = 64


def _conv_kernel(cur_ref, halo_ref, meta_ref, w_ref, b_ref, lg_ref, lb_ref, o_ref,
                 buf_ref, sh_ref, y_ref):
    t = pl.program_id(1)
    ts = cur_ref.shape[0]

    @pl.when(t == 0)
    def _():
        buf_ref[0:CONV_HALO - N_META, :] = jnp.zeros((CONV_HALO - N_META, CONV_WIDTH), F32)
        buf_ref[CONV_HALO - N_META:CONV_HALO, :] = meta_ref[...].astype(F32)

    @pl.when(t > 0)
    def _():
        buf_ref[0:CONV_HALO, :] = halo_ref[...].astype(F32)

    buf_ref[CONV_HALO:CONV_HALO + ts, :] = cur_ref[...].astype(F32)
    first = CONV_HALO - (CONV_KERNEL - 1)

    n_sh = ts + CONV_HALO - SUBLANES
    for s in range(1, SUBLANES):
        sh_ref[s - 1, 0:n_sh, :] = buf_ref[s:s + n_sh, :]

    def tap(base, j, lanes):
        shift, row0 = (first + j) % SUBLANES, base + (first + j) // SUBLANES * SUBLANES
        src = buf_ref if shift == 0 else sh_ref.at[shift - 1]
        return src[row0:row0 + CONV_ROWS, lanes] * w_ref[j:j + 1, lanes]

    for base in range(0, ts, CONV_ROWS):
        for c in range(CONV_WIDTH // LANES):
            lanes = slice(c * LANES, (c + 1) * LANES)
            acc = tap(base, 0, lanes)
            for j in range(1, CONV_KERNEL):
                acc = acc + tap(base, j, lanes)
            y_ref[base:base + CONV_ROWS, lanes] = acc + b_ref[:, lanes]

    y = y_ref[...]
    mu = jnp.mean(y, axis=-1, keepdims=True)
    yc = y - mu
    var = jnp.mean(yc * yc, axis=-1, keepdims=True)
    yn = yc * lax.rsqrt(var + EPS) * lg_ref[...] + lb_ref[...]
    o_ref[...] = (yn * _sigmoid(yn)).astype(BF16)


def _conv_call(packed, meta_packed, dw, db, lg, lb, *, batch, seq, ts):
    nt = seq // ts
    glu_blk = (3 * QK_WIDTH) // CONV_WIDTH
    meta_blk = (2 * QK_WIDTH) // CONV_WIDTH
    vec = pl.BlockSpec((1, CONV_WIDTH), lambda b, t: (0, 0))

    def halo_map(b, t):
        return (jnp.maximum((b * seq + t * ts) // CONV_HALO - 1, 0), glu_blk)

    return pl.pallas_call(
        _conv_kernel,
        grid=(batch, nt),
        in_specs=[
            pl.BlockSpec((ts, CONV_WIDTH), lambda b, t: (b * nt + t, glu_blk)),
            pl.BlockSpec((CONV_HALO, CONV_WIDTH), halo_map),
            pl.BlockSpec((N_META, CONV_WIDTH), lambda b, t: (0, meta_blk)),
            pl.BlockSpec((CONV_KERNEL, CONV_WIDTH), lambda b, t: (0, 0)),
            vec, vec, vec,
        ],
        out_specs=pl.BlockSpec((ts, CONV_WIDTH), lambda b, t: (b * nt + t, 0)),
        out_shape=jax.ShapeDtypeStruct((batch * seq, CONV_WIDTH), BF16),
        scratch_shapes=[pltpu.VMEM((CONV_HALO + ts, CONV_WIDTH), F32),
                        pltpu.VMEM((SUBLANES - 1, CONV_HALO + ts, CONV_WIDTH), F32),
                        pltpu.VMEM((ts, CONV_WIDTH), F32)],
        compiler_params=pltpu.CompilerParams(
            dimension_semantics=("arbitrary", "arbitrary"), vmem_limit_bytes=VMEM_LIMIT),
        name="conv",
    )(packed, packed, meta_packed, dw, db, lg, lb)


def _merge_kernel(o_ref, c_ref, sga_ref, sgc_ref, x_ref, wa_ref, wc_ref, wo_ref, g2_ref,
                  h_ref, u2_ref):
    ya = jnp.dot(o_ref[...], wa_ref[...], preferred_element_type=F32)
    yc = jnp.dot(c_ref[...], wc_ref[...], preferred_element_type=F32)
    mix = (sga_ref[...].astype(F32) * ya + sgc_ref[...].astype(F32) * yc).astype(BF16)
    h = x_ref[...] + jnp.dot(mix, wo_ref[...], preferred_element_type=F32)
    h_ref[...] = h
    ms = jnp.mean(h * h, axis=-1, keepdims=True)
    u2_ref[...] = (h * lax.rsqrt(ms + EPS) * g2_ref[...]).astype(BF16)


def _merge_call(o, c, packed, x2, wa, wc, wo, g2, *, tm):
    m = x2.shape[0]
    gate_blk = (3 * QK_WIDTH + CONV_WIDTH) // D_MODEL

    def const(shape):
        return pl.BlockSpec(shape, lambda i: (0, 0), pipeline_mode=pl.Buffered(1))

    return pl.pallas_call(
        _merge_kernel,
        grid=(m // tm,),
        in_specs=[
            pl.BlockSpec((tm, ATTN_WIDTH), lambda i: (i, 0)),
            pl.BlockSpec((tm, CONV_WIDTH), lambda i: (i, 0)),
            pl.BlockSpec((tm, D_MODEL), lambda i: (i, gate_blk)),
            pl.BlockSpec((tm, D_MODEL), lambda i: (i, gate_blk + 1)),
            pl.BlockSpec((tm, D_MODEL), lambda i: (i, 0)),
            const((ATTN_WIDTH, D_MODEL)),
            const((CONV_WIDTH, D_MODEL)),
            const((D_MODEL, D_MODEL)),
            const((1, D_MODEL)),
        ],
        out_specs=[pl.BlockSpec((tm, D_MODEL), lambda i: (i, 0)),
                   pl.BlockSpec((tm, D_MODEL), lambda i: (i, 0))],
        out_shape=[jax.ShapeDtypeStruct((m, D_MODEL), F32),
                   jax.ShapeDtypeStruct((m, D_MODEL), BF16)],
        compiler_params=pltpu.CompilerParams(
            dimension_semantics=("arbitrary",), vmem_limit_bytes=VMEM_LIMIT),
        name="merge",
    )(o, c, packed, packed, x2, wa, wc, wo, g2)


def _mlp_kernel(u_ref, h_ref, wu_ref, wd_ref, o_ref, acc_ref):
    f = pl.program_id(1)

    @pl.when(f == 0)
    def _():
        acc_ref[...] = h_ref[...]

    z = jnp.maximum(jnp.dot(u_ref[...], wu_ref[...], preferred_element_type=F32), 0.0)
    z = (z * z).astype(BF16)
    acc_ref[...] += jnp.dot(z, wd_ref[...], preferred_element_type=F32)

    @pl.when(f == pl.num_programs(1) - 1)
    def _():
        o_ref[...] = acc_ref[...]


def _mlp_call(u2, h, wu, wd, *, tm, tf):
    m = u2.shape[0]
    return pl.pallas_call(
        _mlp_kernel,
        grid=(m // tm, D_FF // tf),
        in_specs=[
            pl.BlockSpec((tm, D_MODEL), lambda i, f: (i, 0)),
            pl.BlockSpec((tm, D_MODEL), lambda i, f: (i, 0)),
            pl.BlockSpec((None, D_MODEL, tf), lambda i, f: (f, 0, 0)),
            pl.BlockSpec((tf, D_MODEL), lambda i, f: (f, 0)),
        ],
        out_specs=pl.BlockSpec((tm, D_MODEL), lambda i, f: (i, 0)),
        out_shape=jax.ShapeDtypeStruct((m, D_MODEL), F32),
        scratch_shapes=[pltpu.VMEM((tm, D_MODEL), F32)],
        compiler_params=pltpu.CompilerParams(
            dimension_semantics=("arbitrary", "arbitrary"), vmem_limit_bytes=VMEM_LIMIT),
        name="mlp",
    )(u2, h, wu, wd)


def _rope_tables(length):
    pos = jnp.arange(length, dtype=F32)
    inv_freq = ROPE_THETA ** (-jnp.arange(0, ROT_DIM, 2, dtype=F32) / ROT_DIM)
    ang = pos[:, None] * inv_freq[None, :]
    half = ROT_DIM // 2
    lane = jnp.arange(LANES) % HEAD_DIM
    ang_l = ang[:, jnp.arange(LANES) % half]
    cos_t = jnp.where(lane < ROT_DIM, jnp.cos(ang_l), 1.0)
    sin_l = jnp.sin(ang_l)
    sa_t = jnp.where((lane >= half) & (lane < ROT_DIM), sin_l, 0.0)
    sb_t = jnp.where(lane < half, -sin_l, 0.0)
    return cos_t.astype(F32), sa_t.astype(F32), sb_t.astype(F32)


def kernel(x, meta, norm1_g, w_in, q_norm_g, k_norm_g, lambda_q1, lambda_k1, lambda_q2, lambda_k2,
           subln_g, w_attn_o, dw_kernel, dw_bias, conv_ln_g, conv_ln_b, w_conv_o, w_out, norm2_g,
           w_up, w_down):
    b, s, d = x.shape
    x2 = x.reshape(b * s, d)
    l = 0

    w_in_bf = w_in[l].astype(BF16).reshape(d, -1, PROJ_TN).transpose(1, 0, 2)
    wa = w_attn_o[l].astype(BF16)
    wc = w_conv_o[l].astype(BF16)
    wo = w_out[l].astype(BF16)
    wu = w_up[l].astype(BF16)
    wd = w_down[l].astype(BF16)

    qkg = jnp.concatenate([jnp.tile(q_norm_g[l] * (HEAD_DIM ** -0.5 * math.log2(math.e)),
                                    QK_WIDTH // HEAD_DIM),
                           jnp.tile(k_norm_g[l], QK_WIDTH // HEAD_DIM)])[None, :]
    cos_t, sa_t, sb_t = _rope_tables(N_META + s)
    grp = jnp.arange(LANES) // HEAD_DIM
    gsum = (grp[:, None] == grp[None, :]).astype(BF16)
    g1 = norm1_g[l][None, :]

    packed = _proj_call(x2, g1, w_in_bf, qkg, cos_t[N_META:], sa_t[N_META:], sb_t[N_META:], gsum,
                        tm=1024, j0=0, nj=N_PROJ_TILES)
    meta_packed = _proj_call(meta, g1, w_in_bf, qkg, cos_t[:N_META], sa_t[:N_META], sb_t[:N_META],
                             gsum, tm=N_META, j0=META_TILE0, nj=N_META_TILES)

    o = _attn_call(packed, meta_packed, lambda_q1[l][None, :], lambda_k1[l][None, :],
                   lambda_q2[l][None, :], lambda_k2[l][None, :], subln_g[l][None, :],
                   batch=b, seq=s, tq=256, heads=2)
    c = _conv_call(packed, meta_packed, dw_kernel[l], dw_bias[l][None, :], conv_ln_g[l][None, :],
                   conv_ln_b[l][None, :], batch=b, seq=s, ts=256)
    h, u2 = _merge_call(o, c, packed, x2, wa, wc, wo, norm2_g[l][None, :], tm=512)
    tf = 1024
    out = _mlp_call(u2, h, wu.reshape(d, -1, tf).transpose(1, 0, 2), wd, tm=512, tf=tf)
    return out.reshape(b, s, d)
```

```python
import functools
import math

import jax
import jax.numpy as jnp
from jax import lax
from jax.experimental import pallas as pl
from jax.experimental.pallas import tpu as pltpu

F32 = jnp.float32
BF16 = jnp.bfloat16

D_MODEL = 2048
N_META = 16
HEADS = 8
HEAD_DIM = 64
V_HEAD_DIM = 2 * HEAD_DIM
QK_WIDTH = HEADS * 2 * HEAD_DIM
ATTN_WIDTH = HEADS * V_HEAD_DIM
ROT_DIM = HEAD_DIM // 4
ROPE_THETA = 500000.0
CONV_WIDTH = D_MODEL // 2
CONV_KERNEL = 31
D_FF = 4 * D_MODEL
EPS = 1e-6
LAM_INIT = 0.8 - 0.6 * math.exp(-0.3 * 0)

LANES = 128
SUBLANES = 8
ATTN_ACC_PAD = 16
VMEM_LIMIT = 56 * 1024 * 1024

PROJ_TN = 1024
PACKED_COLS = 3 * QK_WIDTH + CONV_WIDTH + 2 * D_MODEL
N_PROJ_TILES = PACKED_COLS // PROJ_TN
N_GLU_TILES = CONV_WIDTH // PROJ_TN
GLU_TILE0 = (3 * QK_WIDTH) // PROJ_TN
GATE_TILE0 = GLU_TILE0 + N_GLU_TILES
GLU_B_WTILE0 = (3 * QK_WIDTH + CONV_WIDTH) // PROJ_TN
N_QK_TILES = (2 * QK_WIDTH) // PROJ_TN
META_TILE0 = QK_WIDTH // PROJ_TN
N_META_TILES = GATE_TILE0 - META_TILE0


def _sigmoid(x):
    return 1.0 / (1.0 + jnp.exp(-x))


def _proj_kernel(x_ref, g_ref, w1_ref, w2_ref, qkg_ref, cos_ref, sa_ref, sb_ref, gsum_ref,
                 o_ref, u_ref, *, j0, row_chunk):
    j = pl.program_id(1) + j0
    tm = x_ref.shape[0]

    @pl.when(pl.program_id(1) == 0)
    def _():
        def body(r, carry):
            rows = pl.ds(pl.multiple_of(r * row_chunk, row_chunk), row_chunk)
            x = x_ref[rows, :]
            ms = jnp.mean(x * x, axis=-1, keepdims=True)
            u_ref[rows, :] = (x * lax.rsqrt(ms + EPS) * g_ref[...]).astype(BF16)
            return carry
        lax.fori_loop(0, tm // row_chunk, body, 0)

    @pl.when(j < N_QK_TILES)
    def _():
        y = jnp.dot(u_ref[...], w1_ref[...], preferred_element_type=F32)
        for c in range(PROJ_TN // LANES):
            lanes = slice(c * LANES, (c + 1) * LANES)
            yc = y[:, lanes]
            yy = yc * yc
            hi = yy.astype(BF16)
            lo = (yy - hi.astype(F32)).astype(BF16)
            ss = (jnp.dot(hi, gsum_ref[...], preferred_element_type=F32)
                  + jnp.dot(lo, gsum_ref[...], preferred_element_type=F32))
            yn = yc * lax.rsqrt(ss * (1.0 / HEAD_DIM) + EPS) * qkg_ref[:, lanes]
            out = (yn * cos_ref[...]
                   + pltpu.roll(yn, ROT_DIM // 2, 1) * sa_ref[...]
                   + pltpu.roll(yn, LANES - ROT_DIM // 2, 1) * sb_ref[...])
            o_ref[:, lanes] = out.astype(BF16)

    @pl.when((j >= N_QK_TILES) & (j < GLU_TILE0))
    def _():
        y = jnp.dot(u_ref[...], w1_ref[...], preferred_element_type=F32)
        o_ref[...] = y.astype(BF16)

    @pl.when((j >= GLU_TILE0) & (j < GATE_TILE0))
    def _():
        ya = jnp.dot(u_ref[...], w1_ref[...], preferred_element_type=F32)
        yb = jnp.dot(u_ref[...], w2_ref[...], preferred_element_type=F32)
        o_ref[...] = (ya * _sigmoid(yb)).astype(BF16)

    @pl.when(j >= GATE_TILE0)
    def _():
        y = jnp.dot(u_ref[...], w1_ref[...], preferred_element_type=F32)
        o_ref[...] = _sigmoid(y).astype(BF16)


def _proj_call(x2, g, w_in_bf, qkg, cos_t, sa_t, sb_t, gsum, *, tm, j0, nj):
    m = x2.shape[0]
    npb = cos_t.shape[0] // tm

    def w1_map(i, j):
        jj = j + j0
        return (0, jnp.where(jj < GATE_TILE0, jj, jj + N_GLU_TILES))

    def w2_map(i, j):
        return (0, jnp.clip(j + j0 + N_GLU_TILES, GLU_B_WTILE0, GLU_B_WTILE0 + N_GLU_TILES - 1))

    def qkg_map(i, j):
        return (0, jnp.minimum(j + j0, N_QK_TILES - 1))

    kern = functools.partial(_proj_kernel, j0=j0, row_chunk=min(tm, 64))
    return pl.pallas_call(
        kern,
        grid=(m // tm, nj),
        in_specs=[
            pl.BlockSpec((tm, D_MODEL), lambda i, j: (i, 0)),
            pl.BlockSpec((1, D_MODEL), lambda i, j: (0, 0)),
            pl.BlockSpec((D_MODEL, PROJ_TN), w1_map),
            pl.BlockSpec((D_MODEL, PROJ_TN), w2_map),
            pl.BlockSpec((1, PROJ_TN), qkg_map),
            pl.BlockSpec((tm, LANES), lambda i, j: (i % npb, 0)),
            pl.BlockSpec((tm, LANES), lambda i, j: (i % npb, 0)),
            pl.BlockSpec((tm, LANES), lambda i, j: (i % npb, 0)),
            pl.BlockSpec((LANES, LANES), lambda i, j: (0, 0)),
        ],
        out_specs=pl.BlockSpec((tm, PROJ_TN), lambda i, j: (i, j)),
        out_shape=jax.ShapeDtypeStruct((m, nj * PROJ_TN), BF16),
        scratch_shapes=[pltpu.VMEM((tm, D_MODEL), BF16)],
        compiler_params=pltpu.CompilerParams(
            dimension_semantics=("arbitrary", "arbitrary"), vmem_limit_bytes=VMEM_LIMIT),
        name="proj",
    )(x2, g, w_in_bf, w_in_bf, qkg, cos_t, sa_t, sb_t, gsum)


def _attn_kernel(q_ref, k_ref, v_ref, km_ref, vmt_ref, lq1_ref, lk1_ref, lq2_ref, lk2_ref, sg_ref,
                 o_ref, qt_ref, vt_ref, qq_ref, st_ref, acc_ref, m_ref, *, tq, heads):
    seq = q_ref.shape[0]
    nq = seq // tq
    lam = (jnp.exp(jnp.sum(lq1_ref[...] * lk1_ref[...], axis=-1, keepdims=True))
           - jnp.exp(jnp.sum(lq2_ref[...] * lk2_ref[...], axis=-1, keepdims=True))
           + LAM_INIT)
    k_io = lax.broadcasted_iota(jnp.int32, (tq, 2 * tq), 0)
    q_io = lax.broadcasted_iota(jnp.int32, (tq, 2 * tq), 1)
    causal = k_io <= jnp.where(q_io >= tq, q_io - tq, q_io)
    head_lanes = [slice(g * V_HEAD_DIM, (g + 1) * V_HEAD_DIM) for g in range(heads)]
    pad_io = lax.broadcasted_iota(jnp.int32, (ATTN_ACC_PAD, tq), 0)
    ones_row = jnp.where(pad_io == 0, 1.0, 0.0).astype(BF16)
    meta_pad = jnp.zeros((LANES - N_META, 2 * tq), BF16)

    for g in range(heads):
        for c in range(nq):
            rows = slice(c * tq, (c + 1) * tq)
            qt_ref[g, c] = q_ref[rows, head_lanes[g]].astype(F32).T.astype(BF16)
            vt_ref[g, c, :V_HEAD_DIM] = v_ref[rows, head_lanes[g]].astype(F32).T.astype(BF16)
            vt_ref[g, c, V_HEAD_DIM:] = ones_row

    def scores(g, kb, slot):
        st_ref[g, slot] = jnp.dot(kb, qq_ref[g], preferred_element_type=F32)

    def update(g, slot, vt, mask):
        st = st_ref[g, slot]
        if mask is not None:
            st = jnp.where(mask, st, -jnp.inf)
        m_old = m_ref[g]
        m_new = jnp.maximum(m_old, jnp.max(st, axis=0, keepdims=True))
        alpha = jnp.exp2(m_old - m_new)
        p = jnp.exp2(st - m_new).astype(BF16)
        acc_ref[g] = alpha * acc_ref[g] + jnp.dot(vt, p, preferred_element_type=F32)
        m_ref[g] = m_new

    zeros = jnp.zeros((HEAD_DIM, tq), BF16)

    def full_block(t, slot):
        nxt = pl.ds(pl.multiple_of((t + 1) * tq, tq), tq)
        for g in range(heads):
            scores(g, k_ref[nxt, head_lanes[g]], 1 - slot)
        for g in range(heads):
            update(g, slot, vt_ref[g, t], None)

    def query_block(qi, carry):
        for g in range(heads):
            qt = qt_ref[g, qi]
            qq_ref[g, :, :tq] = jnp.concatenate([qt[:HEAD_DIM], zeros], axis=0)
            qq_ref[g, :, tq:] = jnp.concatenate([zeros, qt[HEAD_DIM:]], axis=0)
            scores(g, k_ref[0:tq, head_lanes[g]], 0)

            st = jnp.dot(km_ref[:, head_lanes[g]], qq_ref[g], preferred_element_type=F32)
            m = jnp.max(st, axis=0, keepdims=True)
            p = jnp.concatenate([jnp.exp2(st - m).astype(BF16), meta_pad], axis=0)
            m_ref[g] = m
            acc_ref[g] = jnp.dot(vmt_ref[g], p, preferred_element_type=F32)

        def pair(tt, c):
            full_block(2 * tt, 0)
            full_block(2 * tt + 1, 1)
            return c
        lax.fori_loop(0, qi // 2, pair, 0)
        odd = lax.rem(qi, 2)

        def last_full(_, c):
            full_block(qi - 1, 0)
            return c
        lax.fori_loop(0, odd, last_full, 0)

        @pl.when(odd == 0)
        def _():
            for g in range(heads):
                update(g, 0, vt_ref[g, qi], causal)

        @pl.when(odd == 1)
        def _():
            for g in range(heads):
                update(g, 1, vt_ref[g, qi], causal)

        rows = pl.ds(pl.multiple_of(qi * tq, tq), tq)
        for g in range(heads):
            acc = acc_ref[g]
            on = acc[:V_HEAD_DIM] * (1.0 / acc[V_HEAD_DIM:V_HEAD_DIM + 1])
            ot = on[:, :tq] - lam * on[:, tq:]
            ms = jnp.mean(ot * ot, axis=0, keepdims=True)
            o = (ot * lax.rsqrt(ms + EPS)).T * sg_ref[...] * (1.0 - LAM_INIT)
            o_ref[rows, head_lanes[g]] = o.astype(BF16)
        return carry
    lax.fori_loop(0, nq, query_block, 0)


def _attn_call(packed, meta_packed, lq1, lk1, lq2, lk2, subln_g, *, batch, seq, tq, heads):
    vec = pl.BlockSpec((1, HEAD_DIM), lambda b, h: (0, 0))
    nq = seq // tq
    width = heads * V_HEAD_DIM
    ng = HEADS // heads
    return pl.pallas_call(
        functools.partial(_attn_kernel, tq=tq, heads=heads),
        scratch_shapes=[
            pltpu.VMEM((heads, nq, V_HEAD_DIM, tq), BF16),
            pltpu.VMEM((heads, nq, V_HEAD_DIM + ATTN_ACC_PAD, tq), BF16),
            pltpu.VMEM((heads, V_HEAD_DIM, 2 * tq), BF16),
            pltpu.VMEM((heads, 2, tq, 2 * tq), F32),
            pltpu.VMEM((heads, V_HEAD_DIM + ATTN_ACC_PAD, 2 * tq), F32),
            pltpu.VMEM((heads, 1, 2 * tq), F32),
        ],
        grid=(batch, ng),
        in_specs=[
            pl.BlockSpec((seq, width), lambda b, h: (b, h)),
            pl.BlockSpec((seq, width), lambda b, h: (b, ng + h)),
            pl.BlockSpec((seq, width), lambda b, h: (b, 2 * ng + h)),
            pl.BlockSpec((N_META, width), lambda b, h: (0, h)),
            pl.BlockSpec((heads, V_HEAD_DIM + ATTN_ACC_PAD, LANES), lambda b, h: (h, 0, 0)),
            vec, vec, vec, vec,
            pl.BlockSpec((1, V_HEAD_DIM), lambda b, h: (0, 0)),
        ],
        out_specs=pl.BlockSpec((seq, width), lambda b, h: (b, h)),
        out_shape=jax.ShapeDtypeStruct((batch * seq, ATTN_WIDTH), BF16),
        compiler_params=pltpu.CompilerParams(
            dimension_semantics=("arbitrary", "arbitrary"), vmem_limit_bytes=VMEM_LIMIT),
        name="attn",
    )(packed, packed, packed, meta_packed, _meta_values_transposed(meta_packed),
      lq1, lk1, lq2, lk2, subln_g)


def _meta_values_transposed(meta_packed):
    vm = meta_packed[:, QK_WIDTH:QK_WIDTH + ATTN_WIDTH].reshape(N_META, HEADS, V_HEAD_DIM)
    vmt = jnp.pad(vm.transpose(1, 2, 0), ((0, 0), (0, 0), (0, LANES - N_META)))
    ones = (jnp.arange(LANES) < N_META).astype(BF16)
    den = jnp.zeros((HEADS, ATTN_ACC_PAD, LANES), BF16).at[:, 0, :].set(ones)
    return jnp.concatenate([vmt, den], axis=1)


CONV_HALO = 32
CONV_ROWS = 64


def _conv_kernel(cur_ref, halo_ref, meta_ref, w_ref, b_ref, lg_ref, lb_ref, o_ref,
                 buf_ref, sh_ref, y_ref):
    t = pl.program_id(1)
    ts = cur_ref.shape[0]

    @pl.when(t == 0)
    def _():
        buf_ref[0:CONV_HALO - N_META, :] = jnp.zeros((CONV_HALO - N_META, CONV_WIDTH), F32)
        buf_ref[CONV_HALO - N_META:CONV_HALO, :] = meta_ref[...].astype(F32)

    @pl.when(t > 0)
    def _():
        buf_ref[0:CONV_HALO, :] = halo_ref[...].astype(F32)

    buf_ref[CONV_HALO:CONV_HALO + ts, :] = cur_ref[...].astype(F32)
    first = CONV_HALO - (CONV_KERNEL - 1)

    n_sh = ts + CONV_HALO - SUBLANES
    for s in range(1, SUBLANES):
        sh_ref[s - 1, 0:n_sh, :] = buf_ref[s:s + n_sh, :]

    def tap(base, j, lanes):
        shift, row0 = (first + j) % SUBLANES, base + (first + j) // SUBLANES * SUBLANES
        src = buf_ref if shift == 0 else sh_ref.at[shift - 1]
        return src[row0:row0 + CONV_ROWS, lanes] * w_ref[j:j + 1, lanes]

    for base in range(0, ts, CONV_ROWS):
        for c in range(CONV_WIDTH // LANES):
            lanes = slice(c * LANES, (c + 1) * LANES)
            acc = tap(base, 0, lanes)
            for j in range(1, CONV_KERNEL):
                acc = acc + tap(base, j, lanes)
            y_ref[base:base + CONV_ROWS, lanes] = acc + b_ref[:, lanes]

    y = y_ref[...]
    mu = jnp.mean(y, axis=-1, keepdims=True)
    yc = y - mu
    var = jnp.mean(yc * yc, axis=-1, keepdims=True)
    yn = yc * lax.rsqrt(var + EPS) * lg_ref[...] + lb_ref[...]
    o_ref[...] = (yn * _sigmoid(yn)).astype(BF16)


def _conv_call(packed, meta_packed, dw, db, lg, lb, *, batch, seq, ts):
    nt = seq // ts
    glu_blk = (3 * QK_WIDTH) // CONV_WIDTH
    meta_blk = (2 * QK_WIDTH) // CONV_WIDTH
    vec = pl.BlockSpec((1, CONV_WIDTH), lambda b, t: (0, 0))

    def halo_map(b, t):
        return (jnp.maximum((b * seq + t * ts) // CONV_HALO - 1, 0), glu_blk)

    return pl.pallas_call(
        _conv_kernel,
        grid=(batch, nt),
        in_specs=[
            pl.BlockSpec((ts, CONV_WIDTH), lambda b, t: (b * nt + t, glu_blk)),
            pl.BlockSpec((CONV_HALO, CONV_WIDTH), halo_map),
            pl.BlockSpec((N_META, CONV_WIDTH), lambda b, t: (0, meta_blk)),
            pl.BlockSpec((CONV_KERNEL, CONV_WIDTH), lambda b, t: (0, 0)),
            vec, vec, vec,
        ],
        out_specs=pl.BlockSpec((ts, CONV_WIDTH), lambda b, t: (b * nt + t, 0)),
        out_shape=jax.ShapeDtypeStruct((batch * seq, CONV_WIDTH), BF16),
        scratch_shapes=[pltpu.VMEM((CONV_HALO + ts, CONV_WIDTH), F32),
                        pltpu.VMEM((SUBLANES - 1, CONV_HALO + ts, CONV_WIDTH), F32),
                        pltpu.VMEM((ts, CONV_WIDTH), F32)],
        compiler_params=pltpu.CompilerParams(
            dimension_semantics=("arbitrary", "arbitrary"), vmem_limit_bytes=VMEM_LIMIT),
        name="conv",
    )(packed, packed, meta_packed, dw, db, lg, lb)


def _merge_kernel(o_ref, c_ref, sga_ref, sgc_ref, x_ref, wa_ref, wc_ref, wo_ref, g2_ref,
                  h_ref, u2_ref):
    ya = jnp.dot(o_ref[...], wa_ref[...], preferred_element_type=F32)
    yc = jnp.dot(c_ref[...], wc_ref[...], preferred_element_type=F32)
    mix = (sga_ref[...].astype(F32) * ya + sgc_ref[...].astype(F32) * yc).astype(BF16)
    h = x_ref[...] + jnp.dot(mix, wo_ref[...], preferred_element_type=F32)
    h_ref[...] = h
    ms = jnp.mean(h * h, axis=-1, keepdims=True)
    u2_ref[...] = (h * lax.rsqrt(ms + EPS) * g2_ref[...]).astype(BF16)


def _merge_call(o, c, packed, x2, wa, wc, wo, g2, *, tm):
    m = x2.shape[0]
    gate_blk = (3 * QK_WIDTH + CONV_WIDTH) // D_MODEL

    def const(shape):
        return pl.BlockSpec(shape, lambda i: (0, 0), pipeline_mode=pl.Buffered(1))

    return pl.pallas_call(
        _merge_kernel,
        grid=(m // tm,),
        in_specs=[
            pl.BlockSpec((tm, ATTN_WIDTH), lambda i: (i, 0)),
            pl.BlockSpec((tm, CONV_WIDTH), lambda i: (i, 0)),
            pl.BlockSpec((tm, D_MODEL), lambda i: (i, gate_blk)),
            pl.BlockSpec((tm, D_MODEL), lambda i: (i, gate_blk + 1)),
            pl.BlockSpec((tm, D_MODEL), lambda i: (i, 0)),
            const((ATTN_WIDTH, D_MODEL)),
            const((CONV_WIDTH, D_MODEL)),
            const((D_MODEL, D_MODEL)),
            const((1, D_MODEL)),
        ],
        out_specs=[pl.BlockSpec((tm, D_MODEL), lambda i: (i, 0)),
                   pl.BlockSpec((tm, D_MODEL), lambda i: (i, 0))],
        out_shape=[jax.ShapeDtypeStruct((m, D_MODEL), F32),
                   jax.ShapeDtypeStruct((m, D_MODEL), BF16)],
        compiler_params=pltpu.CompilerParams(
            dimension_semantics=("arbitrary",), vmem_limit_bytes=VMEM_LIMIT),
        name="merge",
    )(o, c, packed, packed, x2, wa, wc, wo, g2)


def _mlp_kernel(u_ref, h_ref, wu_ref, wd_ref, o_ref, acc_ref):
    f = pl.program_id(1)

    @pl.when(f == 0)
    def _():
        acc_ref[...] = h_ref[...]

    z = jnp.maximum(jnp.dot(u_ref[...], wu_ref[...], preferred_element_type=F32), 0.0)
    z = (z * z).astype(BF16)
    acc_ref[...] += jnp.dot(z, wd_ref[...], preferred_element_type=F32)

    @pl.when(f == pl.num_programs(1) - 1)
    def _():
        o_ref[...] = acc_ref[...]


def _mlp_call(u2, h, wu, wd, *, tm, tf):
    m = u2.shape[0]
    return pl.pallas_call(
        _mlp_kernel,
        grid=(m // tm, D_FF // tf),
        in_specs=[
            pl.BlockSpec((tm, D_MODEL), lambda i, f: (i, 0)),
            pl.BlockSpec((tm, D_MODEL), lambda i, f: (i, 0)),
            pl.BlockSpec((D_MODEL, tf), lambda i, f: (0, f)),
            pl.BlockSpec((tf, D_MODEL), lambda i, f: (f, 0)),
        ],
        out_specs=pl.BlockSpec((tm, D_MODEL), lambda i, f: (i, 0)),
        out_shape=jax.ShapeDtypeStruct((m, D_MODEL), F32),
        scratch_shapes=[pltpu.VMEM((tm, D_MODEL), F32)],
        compiler_params=pltpu.CompilerParams(
            dimension_semantics=("arbitrary", "arbitrary"), vmem_limit_bytes=VMEM_LIMIT),
        name="mlp",
    )(u2, h, wu, wd)


def _rope_tables(length):
    pos = jnp.arange(length, dtype=F32)
    inv_freq = ROPE_THETA ** (-jnp.arange(0, ROT_DIM, 2, dtype=F32) / ROT_DIM)
    ang = pos[:, None] * inv_freq[None, :]
    half = ROT_DIM // 2
    lane = jnp.arange(LANES) % HEAD_DIM
    ang_l = ang[:, jnp.arange(LANES) % half]
    cos_t = jnp.where(lane < ROT_DIM, jnp.cos(ang_l), 1.0)
    sin_l = jnp.sin(ang_l)
    sa_t = jnp.where((lane >= half) & (lane < ROT_DIM), sin_l, 0.0)
    sb_t = jnp.where(lane < half, -sin_l, 0.0)
    return cos_t.astype(F32), sa_t.astype(F32), sb_t.astype(F32)


def kernel(x, meta, norm1_g, w_in, q_norm_g, k_norm_g, lambda_q1, lambda_k1, lambda_q2, lambda_k2,
           subln_g, w_attn_o, dw_kernel, dw_bias, conv_ln_g, conv_ln_b, w_conv_o, w_out, norm2_g,
           w_up, w_down):
    b, s, d = x.shape
    x2 = x.reshape(b * s, d)
    l = 0

    w_in_bf = w_in[l].astype(BF16)
    wa = w_attn_o[l].astype(BF16)
    wc = w_conv_o[l].astype(BF16)
    wo = w_out[l].astype(BF16)
    wu = w_up[l].astype(BF16)
    wd = w_down[l].astype(BF16)

    qkg = jnp.concatenate([jnp.tile(q_norm_g[l] * (HEAD_DIM ** -0.5 * math.log2(math.e)),
                                    QK_WIDTH // HEAD_DIM),
                           jnp.tile(k_norm_g[l], QK_WIDTH // HEAD_DIM)])[None, :]
    cos_t, sa_t, sb_t = _rope_tables(N_META + s)
    grp = jnp.arange(LANES) // HEAD_DIM
    gsum = (grp[:, None] == grp[None, :]).astype(BF16)
    g1 = norm1_g[l][None, :]

    packed = _proj_call(x2, g1, w_in_bf, qkg, cos_t[N_META:], sa_t[N_META:], sb_t[N_META:], gsum,
                        tm=1024, j0=0, nj=N_PROJ_TILES)
    meta_packed = _proj_call(meta, g1, w_in_bf, qkg, cos_t[:N_META], sa_t[:N_META], sb_t[:N_META],
                             gsum, tm=N_META, j0=META_TILE0, nj=N_META_TILES)

    o = _attn_call(packed, meta_packed, lambda_q1[l][None, :], lambda_k1[l][None, :],
                   lambda_q2[l][None, :], lambda_k2[l][None, :], subln_g[l][None, :],
                   batch=b, seq=s, tq=256, heads=4)
    c = _conv_call(packed, meta_packed, dw_kernel[l], dw_bias[l][None, :], conv_ln_g[l][None, :],
                   conv_ln_b[l][None, :], batch=b, seq=s, ts=256)
    h, u2 = _merge_call(o, c, packed, x2, wa, wc, wo, norm2_g[l][None, :], tm=512)
    out = _mlp_call(u2, h, wu, wd, tm=512, tf=1024)
    return out.reshape(b, s, d)
```
